```python
import math
import jax, jax.numpy as jnp
from jax import lax
import numpy as np

D_MODEL = 2048
BATCH = 4
SEQ = 4096
DEPTH = 1

D_MIX = D_MODEL
D_GLA = D_MIX // 2
D_SWA = D_MIX - D_GLA
GLA_HEADS = 4
GLA_DK = D_GLA // 2 // GLA_HEADS
GLA_DV = D_GLA // GLA_HEADS
GLA_KW = GLA_HEADS * GLA_DK
GATE_RANK = 16
GATE_TAU = 16.0
GLA_CHUNK = 64
SWA_HEAD_DIM = 64
SWA_Q_HEADS = D_SWA // SWA_HEAD_DIM
SWA_KV_HEADS = 2
SWA_GROUP = SWA_Q_HEADS // SWA_KV_HEADS
SWA_KVW = SWA_KV_HEADS * SWA_HEAD_DIM
WINDOW = 128
D_FF = 4 * D_MODEL
SPLITS = (GLA_KW, GLA_KW, D_GLA, D_GLA, GATE_RANK, D_SWA, SWA_KVW, SWA_KVW)
D_IN = sum(SPLITS)
ALPHA = (2 * DEPTH) ** 0.25
BETA = (8 * DEPTH) ** -0.25
LN_EPS = 1e-5
RMS_EPS = 1e-5

kernel_name = "hybrid_gla_swa_sink_deepnorm"


def split_points():
    pts, acc = [], 0
    for s in SPLITS[:-1]:
        acc += s
        pts.append(acc)
    return pts


def layer_norm(x, g, b):
    xf = x.astype(jnp.float32)
    mu = jnp.mean(xf, axis=-1, keepdims=True)
    var = jnp.mean(jnp.square(xf - mu), axis=-1, keepdims=True)
    y = (xf - mu) * lax.rsqrt(var + LN_EPS) * g.astype(jnp.float32) + b.astype(jnp.float32)
    return y.astype(x.dtype)


def gla_mixer(q, k, v, gk, g_out, norm_w):
    B, S, H, dk = q.shape
    dv = v.shape[-1]
    C = GLA_CHUNK
    nc = S // C

    def to_chunks(t):
        return t.astype(jnp.float32).reshape(B, nc, C, H, t.shape[-1]).transpose(1, 0, 3, 2, 4)

    qc = to_chunks(q) * (dk ** -0.5)
    kc, vc, gc = to_chunks(k), to_chunks(v), to_chunks(gk)
    bc = jnp.cumsum(gc, axis=3)
    causal = jnp.tril(jnp.ones((C, C), dtype=bool))

    def step(state, inp):
        qb, kb, vb, bb = inp
        o_inter = jnp.einsum('bhcd,bhde->bhce', qb * jnp.exp(bb), state)
        diff = bb[:, :, :, None, :] - bb[:, :, None, :, :]
        decay = jnp.exp(jnp.where(causal[:, :, None], diff, -jnp.inf))
        attn = jnp.einsum('bhijd,bhjd->bhij', qb[:, :, :, None, :] * decay, kb)
        o_intra = jnp.einsum('bhij,bhje->bhie', attn, vb)
        b_last = bb[:, :, -1:, :]
        k_dec = kb * jnp.exp(b_last - bb)
        new_state = state * jnp.exp(b_last[:, :, 0, :, None]) + jnp.einsum('bhcd,bhce->bhde', k_dec, vb)
        return new_state, o_inter + o_intra

    state0 = jnp.zeros((B, H, dk, dv), jnp.float32)
    _, oc = lax.scan(step, state0, (qc, kc, vc, bc))
    o = oc.transpose(1, 0, 3, 2, 4).reshape(B, S, H, dv)
    o = o * lax.rsqrt(jnp.mean(jnp.square(o), axis=-1, keepdims=True) + RMS_EPS)
    o = o * norm_w.astype(jnp.float32) * jax.nn.silu(g_out.astype(jnp.float32))
    return o.reshape(B, S, H * dv).astype(q.dtype)


def swa_mixer(q, k, v, sinks):
    B, S, _, dh = q.shape
    W = WINDOW
    nb = S // W
    qb = q.astype(jnp.float32).reshape(B, nb, W, SWA_KV_HEADS, SWA_GROUP, dh)

    def band(t):
        tb = t.astype(jnp.float32).reshape(B, nb, W, SWA_KV_HEADS, dh)
        prev = jnp.concatenate([jnp.zeros_like(tb[:, :1]), tb[:, :-1]], axis=1)
        return jnp.concatenate([prev, tb], axis=2)

    kb, vb = band(k), band(v)
    s = jnp.einsum('bnqhgd,bnkhd->bnhgqk', qb, kb) * (dh ** -0.5)
    qi = jnp.arange(W)[:, None]
    kj = jnp.arange(2 * W)[None, :]
    in_window = (kj > qi) & (kj <= qi + W)
    blk = jnp.arange(nb)[:, None, None]
    valid = in_window[None] & ((blk > 0) | (kj[None] >= W))
    s = jnp.where(valid[None, :, None, None], s, -jnp.inf)
    sink = sinks.astype(jnp.float32).reshape(SWA_KV_HEADS, SWA_GROUP)[None, None, :, :, None, None]
    m = jnp.maximum(jnp.max(s, axis=-1, keepdims=True), sink)
    p = jnp.exp(s - m)
    denom = jnp.sum(p, axis=-1, keepdims=True) + jnp.exp(sink - m)
    o = jnp.einsum('bnhgqk,bnkhd->bnqhgd', p / denom, vb)
    return o.reshape(B, S, SWA_Q_HEADS * dh).astype(q.dtype)


def setup_inputs(seed: int = 0) -> dict:
    key = jax.random.key(seed)
    ks = jax.random.split(key, 12)
    f32 = jnp.float32
    x = jax.random.normal(ks[0], (BATCH, SEQ, D_MODEL), f32)
    col_scale = np.concatenate([np.full((s,), BETA if i in (2, 6, 7) else 1.0, np.float32)
                                for i, s in enumerate(SPLITS)])
    w_in = jax.random.normal(ks[1], (DEPTH, D_MODEL, D_IN), f32) * (D_MODEL ** -0.5) * jnp.asarray(col_scale)
    w_gk2 = jax.random.normal(ks[2], (DEPTH, GATE_RANK, GLA_KW), f32) * (GATE_RANK ** -0.5)
    b_gk = 0.1 * jax.random.normal(ks[3], (DEPTH, GLA_KW), f32)
    gla_norm_w = 1.0 + 0.02 * jax.random.normal(ks[4], (DEPTH, GLA_DV), f32)
    swa_sinks = 0.5 * jax.random.normal(ks[5], (DEPTH, SWA_Q_HEADS), f32)
    w_out = jax.random.normal(ks[6], (DEPTH, D_MIX, D_MODEL), f32) * (D_MIX ** -0.5) * BETA
    ln1_g = 1.0 + 0.02 * jax.random.normal(ks[7], (DEPTH, D_MODEL), f32)
    ln1_b = 0.02 * jax.random.normal(ks[8], (DEPTH, D_MODEL), f32)
    w_up = jax.random.normal(ks[9], (DEPTH, D_MODEL, D_FF), f32) * (D_MODEL ** -0.5)
    w_down = jax.random.normal(ks[10], (DEPTH, D_FF, D_MODEL), f32) * (D_FF ** -0.5) * BETA
    k2 = jax.random.split(ks[11], 2)
    ln2_g = 1.0 + 0.02 * jax.random.normal(k2[0], (DEPTH, D_MODEL), f32)
    ln2_b = 0.02 * jax.random.normal(k2[1], (DEPTH, D_MODEL), f32)
    return {"x": x, "w_in": w_in, "w_gk2": w_gk2, "b_gk": b_gk, "gla_norm_w": gla_norm_w,
            "swa_sinks": swa_sinks, "w_out": w_out, "ln1_g": ln1_g, "ln1_b": ln1_b,
            "w_up": w_up, "w_down": w_down, "ln2_g": ln2_g, "ln2_b": ln2_b}


def reference(x, w_in, w_gk2, b_gk, gla_norm_w, swa_sinks, w_out, ln1_g, ln1_b,
              w_up, w_down, ln2_g, ln2_b):
    B, S, _ = x.shape
    pts = split_points()
    for l in range(DEPTH):
        proj = jnp.einsum('bsd,de->bse', x, w_in[l])
        q_g, k_g, v_g, g_g, gk_lo, q_s, k_s, v_s = jnp.split(proj, pts, axis=-1)
        gk = jax.nn.log_sigmoid((jnp.einsum('bsr,rk->bsk', gk_lo, w_gk2[l]) + b_gk[l]).astype(jnp.float32)) / GATE_TAU
        gla_out = gla_mixer(q_g.reshape(B, S, GLA_HEADS, GLA_DK),
                            k_g.reshape(B, S, GLA_HEADS, GLA_DK),
                            v_g.reshape(B, S, GLA_HEADS, GLA_DV),
                            gk.reshape(B, S, GLA_HEADS, GLA_DK),
                            g_g.reshape(B, S, GLA_HEADS, GLA_DV),
                            gla_norm_w[l])
        swa_out = swa_mixer(q_s.reshape(B, S, SWA_Q_HEADS, SWA_HEAD_DIM),
                            k_s.reshape(B, S, SWA_KV_HEADS, SWA_HEAD_DIM),
                            v_s.reshape(B, S, SWA_KV_HEADS, SWA_HEAD_DIM),
                            swa_sinks[l])
        mix = jnp.einsum('bse,ed->bsd', jnp.concatenate([gla_out, swa_out], axis=-1), w_out[l])
        x = layer_norm(ALPHA * x + mix, ln1_g[l], ln1_b[l])
        hdn = jnp.square(jax.nn.relu(jnp.einsum('bsd,df->bsf', x, w_up[l])))
        ff = jnp.einsum('bsf,fd->bsd', hdn, w_down[l])
        x = layer_norm(ALPHA * x + ff, ln2_g[l], ln2_b[l])
    return x
```

```python
import functools

import jax
import jax.numpy as jnp
from jax import lax
from jax.experimental import pallas as pl
from jax.experimental.pallas import tpu as pltpu

F32 = jnp.float32
BF16 = jnp.bfloat16

D_MODEL = 2048
D_GLA = 1024
GLA_HEADS = 4
GLA_DK = 128
GLA_DV = 256
GLA_KW = GLA_HEADS * GLA_DK
GATE_RANK = 16
GATE_TAU = 16.0
SWA_HEAD_DIM = 64
SWA_Q_HEADS = 16
SWA_KV_HEADS = 2
SWA_GROUP = SWA_Q_HEADS // SWA_KV_HEADS
D_SWA = SWA_Q_HEADS * SWA_HEAD_DIM
SWA_KVW = SWA_KV_HEADS * SWA_HEAD_DIM
WINDOW = 128
D_FF = 4 * D_MODEL
SPLITS = (GLA_KW, GLA_KW, D_GLA, D_GLA, GATE_RANK, D_SWA, SWA_KVW, SWA_KVW)
DEPTH = 1
ALPHA = (2 * DEPTH) ** 0.25
LN_EPS = 1e-5
RMS_EPS = 1e-5

LANES = 128
SUBLANES = 8
VMEM_LIMIT_BYTES = 56 * 1024 * 1024

PROJ_ROWS = 512
GLA_CHUNK = 128
GLA_STEP_ROWS = 512
MLP_ROWS = 512
MLP_FF_TILE = 512
NEG_BIG = -1e30


def _nn(a, b):
    return lax.dot_general(a, b, (((1,), (0,)), ((), ())), preferred_element_type=F32)


def _nt(a, b):
    return lax.dot_general(a, b, (((1,), (1,)), ((), ())), preferred_element_type=F32)


def _tn(a, b):
    return lax.dot_general(a, b, (((0,), (0,)), ((), ())), preferred_element_type=F32)


def _layer_norm(y, g, b):
    mu = jnp.mean(y, axis=-1, keepdims=True)
    d = y - mu
    var = jnp.mean(d * d, axis=-1, keepdims=True)
    return d * lax.rsqrt(var + LN_EPS) * g + b


def _params(*semantics):
    return pltpu.CompilerParams(dimension_semantics=semantics, vmem_limit_bytes=VMEM_LIMIT_BYTES)


def _resident(shape):
    nd = len(shape)
    return pl.BlockSpec(shape, lambda *_: (0,) * nd, pipeline_mode=pl.Buffered(1))


def _in_proj_kernel(x_ref, wqg, wkg, wvg, wgg, wgl, wqs, wks, wvs, wgk2, bgk,
                    qg_o, kg_o, vg_o, gg_o, gk_o, qs_o, ks_o, vs_o):
    xb = x_ref[...].astype(BF16)
    for w, o in ((wqg, qg_o), (wkg, kg_o), (wvg, vg_o), (wgg, gg_o),
                 (wqs, qs_o), (wks, ks_o), (wvs, vs_o)):
        o[...] = _nn(xb, w[...]).astype(o.dtype)
    gl = _nn(xb, wgl[...]).astype(BF16)
    z = _nn(gl, wgk2[...]) + bgk[...]
    gk_o[...] = (jnp.minimum(z, 0.0) - jnp.log1p(jnp.exp(-jnp.abs(z)))) * (1.0 / GATE_TAU)


def _in_proj(x2, w_in, w_gk2, b_gk):
    n = x2.shape[0]
    tm = min(PROJ_ROWS, n)
    bounds = [0]
    for s in SPLITS:
        bounds.append(bounds[-1] + s)
    cols = [w_in[:, bounds[i]:bounds[i + 1]].astype(BF16) for i in range(len(SPLITS))]
    wqg, wkg, wvg, wgg, wgl, wqs, wks, wvs = cols
    wgl = jnp.pad(wgl, ((0, 0), (0, LANES - GATE_RANK)))
    wgk2 = jnp.pad(w_gk2.astype(BF16), ((0, LANES - GATE_RANK), (0, 0)))
    bgk = b_gk.reshape(1, GLA_KW).astype(F32)
    weights = (wqg, wkg, wvg, wgg, wgl, wqs, wks, wvs, wgk2, bgk)
    out_dims = (GLA_KW, GLA_KW, D_GLA, D_GLA, GLA_KW, D_SWA, SWA_KVW, SWA_KVW)
    out_dtypes = (BF16, BF16, BF16, BF16, F32, BF16, BF16, BF16)
    return pl.pallas_call(
        _in_proj_kernel,
        grid=(n // tm,),
        in_specs=[pl.BlockSpec((tm, D_MODEL), lambda i: (i, 0))] + [_resident(w.shape) for w in weights],
        out_specs=[pl.BlockSpec((tm, d), lambda i: (i, 0)) for d in out_dims],
        out_shape=[jax.ShapeDtypeStruct((n, d), t) for d, t in zip(out_dims, out_dtypes)],
        compiler_params=_params("parallel"),
        name="in_proj",
    )(x2, *weights)


def _gla_kernel(q_ref, k_ref, v_ref, gk_ref, g_ref, nw_ref, o_ref, st_ref, b_ref, *, chunk, step_rows):
    c = chunk

    @pl.when(pl.program_id(2) == 0)
    def _():
        st_ref[...] = jnp.zeros_like(st_ref)

    row_cc = lax.broadcasted_iota(jnp.int32, (c, c), 0)
    col_cc = lax.broadcasted_iota(jnp.int32, (c, c), 1)
    tril = (col_cc <= row_cc).astype(BF16)
    eye = row_cc == col_cc
    blk_xor = row_cc ^ col_cc
    row_cd = lax.broadcasted_iota(jnp.int32, (c, GLA_DK), 0)
    sub8 = lax.broadcasted_iota(jnp.int32, (SUBLANES, GLA_DK), 0)
    scale = GLA_DK ** -0.5
    nw = nw_ref[...]

    for ci in range(step_rows // c):
        rows = pl.ds(ci * c, c)
        gk = gk_ref[rows, :]
        gk_hi = gk.astype(BF16)
        gk_lo = (gk - gk_hi.astype(F32)).astype(BF16)
        b = _nn(tril, gk_hi) + _nn(tril, gk_lo)
        b_ref[...] = b
        b_last = b_ref[pl.ds(c - 1, 1), :]
        q = q_ref[rows, :].astype(F32) * scale
        k = k_ref[rows, :].astype(F32)
        v = v_ref[rows, :]

        attn = jnp.where(eye, jnp.sum(q * k, axis=-1, keepdims=True), 0.0)
        s = c // 2
        while s >= 1:
            blk = 2 * s
            second = (row_cd & s) != 0
            if s == 1:
                expo = jnp.where(second, gk, 0.0)
            else:
                if blk >= SUBLANES:
                    pieces = [jnp.broadcast_to(b_ref[pl.ds(r0 + s - 1, 1), :], (blk, GLA_DK))
                              for r0 in range(0, c, blk)]
                else:
                    pieces = []
                    for r0 in range(0, c, SUBLANES):
                        cand = [jnp.broadcast_to(b_ref[pl.ds(r0 + o + s - 1, 1), :], (SUBLANES, GLA_DK))
                                for o in range(0, SUBLANES, blk)]
                        m8 = cand[-1]
                        for idx in range(len(cand) - 2, -1, -1):
                            m8 = jnp.where(sub8 < (idx + 1) * blk, cand[idx], m8)
                        pieces.append(m8)
                mid = pieces[0] if len(pieces) == 1 else jnp.concatenate(pieces, axis=0)
                expo = jnp.where(second, b - mid, mid - b)
            f = jnp.exp(expo)
            qs = jnp.where(second, q * f, 0.0).astype(BF16)
            ks = jnp.where(second, 0.0, k * f).astype(BF16)
            r = _nt(qs, ks)
            attn = attn + (r if blk == c else jnp.where(blk_xor < blk, r, 0.0))
            s //= 2

        st = st_ref[...]
        q_in = (q * jnp.exp(b)).astype(BF16)
        o = _nt(q_in, st.astype(BF16)) + _nn(attn.astype(BF16), v)
        k_out = (k * jnp.exp(b_last - b)).astype(BF16)
        st_ref[...] = st * jnp.exp(b_last) + _tn(v, k_out)

        o = o * lax.rsqrt(jnp.mean(o * o, axis=-1, keepdims=True) + RMS_EPS)
        g = g_ref[rows, :].astype(F32)
        o = o * nw * (g * (1.0 / (1.0 + jnp.exp(-g))))
        o_ref[rows, :] = o.astype(o_ref.dtype)


def _gla(qg, kg, vg, gk, gg, norm_w):
    bsz, seq, _ = qg.shape
    step = min(GLA_STEP_ROWS, seq)
    chunk = min(GLA_CHUNK, step)
    kern = functools.partial(_gla_kernel, chunk=chunk, step_rows=step)
    kspec = pl.BlockSpec((None, step, GLA_DK), lambda b, h, t: (b, t, h))
    vspec = pl.BlockSpec((None, step, GLA_DV), lambda b, h, t: (b, t, h))
    return pl.pallas_call(
        kern,
        grid=(bsz, GLA_HEADS, seq // step),
        in_specs=[kspec, kspec, vspec, kspec, vspec, pl.BlockSpec((1, GLA_DV), lambda b, h, t: (0, 0))],
        out_specs=vspec,
        out_shape=jax.ShapeDtypeStruct((bsz, seq, D_GLA), BF16),
        scratch_shapes=[pltpu.VMEM((GLA_DV, GLA_DK), F32), pltpu.VMEM((chunk, GLA_DK), F32)],
        compiler_params=_params("parallel", "parallel", "arbitrary"),
        name="gla_mixer",
    )(qg, kg, vg, gk, gg, norm_w.reshape(1, GLA_DV).astype(F32))


def _swa_kernel(sink_ref, q_ref, kp_ref, kc_ref, vp_ref, vc_ref, o_ref):
    w = WINDOW
    hd = SWA_HEAD_DIM
    n = pl.program_id(1)
    kf = jnp.concatenate([kp_ref[...], kc_ref[...]], axis=0).astype(F32)
    vf = jnp.concatenate([vp_ref[...], vc_ref[...]], axis=0).astype(F32)
    lane_kv = lax.broadcasted_iota(jnp.int32, (2 * w, LANES), 1)
    qi = lax.broadcasted_iota(jnp.int32, (w, 2 * w), 0)
    kj = lax.broadcasted_iota(jnp.int32, (w, 2 * w), 1)
    first_key = jnp.where(n > 0, 0, w)
    valid = (kj > qi) & (kj <= qi + w) & (kj >= first_key)
    lane_o = lax.broadcasted_iota(jnp.int32, (w, LANES), 1)
    pairs = SWA_GROUP // 2
    for g in range(SWA_KV_HEADS):
        k_rot = kf if g == 0 else pltpu.roll(kf, hd, axis=1)
        v_rot = vf if g == 0 else pltpu.roll(vf, hd, axis=1)
        k_lo = jnp.where(lane_kv < hd, k_rot, 0.0)
        v_lo = jnp.where(lane_kv < hd, v_rot, 0.0)
        k_hi = pltpu.roll(k_lo, hd, axis=1).astype(BF16)
        v_hi = pltpu.roll(v_lo, hd, axis=1).astype(BF16)
        k_lo = k_lo.astype(BF16)
        v_lo = v_lo.astype(BF16)
        qst = jnp.concatenate(
            [q_ref[:, pl.ds((g * pairs + p) * LANES, LANES)] for p in range(pairs)], axis=0)
        s_lo = _nt(qst, k_lo)
        s_hi = _nt(qst, k_hi)
        p_parts, inv_parts = [], []
        for s_all, par in ((s_lo, 0), (s_hi, 1)):
            ps, invs = [], []
            for p in range(pairs):
                sink = sink_ref[2 * (g * pairs + p) + par]
                s = jnp.where(valid, s_all[p * w:(p + 1) * w, :] * (hd ** -0.5), NEG_BIG)
                m = jnp.maximum(jnp.max(s, axis=-1, keepdims=True), sink)
                e = jnp.exp(s - m)
                denom = jnp.sum(e, axis=-1, keepdims=True) + jnp.exp(sink - m)
                ps.append(e.astype(BF16))
                invs.append(1.0 / denom)
            p_parts.append(jnp.concatenate(ps, axis=0))
            inv_parts.append(invs)
        acc = _nn(p_parts[0], v_lo) + _nn(p_parts[1], v_hi)
        for p in range(pairs):
            inv = jnp.where(lane_o < hd, inv_parts[0][p], inv_parts[1][p])
            o_ref[:, pl.ds((g * pairs + p) * LANES, LANES)] = (acc[p * w:(p + 1) * w, :] * inv).astype(o_ref.dtype)


def _swa(qs, ks, vs, sinks):
    bsz, seq, _ = qs.shape
    w = WINDOW
    cur = lambda b, n: (b, n, 0)
    prev = lambda b, n: (b, jnp.maximum(n - 1, 0), 0)
    kv = lambda imap: pl.BlockSpec((None, w, SWA_KVW), imap)
    return pl.pallas_call(
        _swa_kernel,
        grid=(bsz, seq // w),
        in_specs=[pl.BlockSpec(memory_space=pltpu.SMEM),
                  pl.BlockSpec((None, w, D_SWA), cur), kv(prev), kv(cur), kv(prev), kv(cur)],
        out_specs=pl.BlockSpec((None, w, D_SWA), cur),
        out_shape=jax.ShapeDtypeStruct((bsz, seq, D_SWA), BF16),
        compiler_params=_params("parallel", "arbitrary"),
        name="swa_mixer",
    )(sinks.astype(F32), qs, ks, ks, vs, vs)


def _out_proj_kernel(gla_ref, swa_ref, x_ref, wa_ref, wb_ref, g_ref, b_ref, o_ref):
    mix = _nn(gla_ref[...], wa_ref[...]) + _nn(swa_ref[...], wb_ref[...])
    o_ref[...] = _layer_norm(ALPHA * x_ref[...] + mix, g_ref[...], b_ref[...])


def _out_proj(gla, swa, x2, w_out, ln_g, ln_b):
    n = x2.shape[0]
    tm = min(PROJ_ROWS, n)
    wa = w_out[:D_GLA].astype(BF16)
    wb = w_out[D_GLA:].astype(BF16)
    row = lambda d: pl.BlockSpec((tm, d), lambda i: (i, 0))
    return pl.pallas_call(
        _out_proj_kernel,
        grid=(n // tm,),
        in_specs=[row(D_GLA), row(D_SWA), row(D_MODEL), _resident(wa.shape), _resident(wb.shape),
                  _resident((1, D_MODEL)), _resident((1, D_MODEL))],
        out_specs=row(D_MODEL),
        out_shape=jax.ShapeDtypeStruct((n, D_MODEL), F32),
        compiler_params=_params("parallel"),
        name="out_proj_ln",
    )(gla, swa, x2, wa, wb, ln_g.reshape(1, D_MODEL), ln_b.reshape(1, D_MODEL))


def _mlp_kernel(x_ref, wu_ref, wd_ref, g_ref, b_ref, o_ref, xb_ref):
    f = pl.program_id(1)

    @pl.when(f == 0)
    def _():
        xb_ref[...] = x_ref[...].astype(BF16)

    h = jnp.maximum(_nn(xb_ref[...], wu_ref[...]), 0.0)
    part = _nn((h * h).astype(BF16), wd_ref[...])

    @pl.when(f == 0)
    def _():
        o_ref[...] = part

    @pl.when(f > 0)
    def _():
        o_ref[...] += part

    @pl.when(f == pl.num_programs(1) - 1)
    def _():
        o_ref[...] = _layer_norm(ALPHA * x_ref[...] + o_ref[...], g_ref[...], b_ref[...])


def _mlp(x1, w_up, w_down, ln_g, ln_b):
    n = x1.shape[0]
    tm = min(MLP_ROWS, n)
    tf = MLP_FF_TILE
    return pl.pallas_call(
        _mlp_kernel,
        grid=(n // tm, D_FF // tf),
        in_specs=[pl.BlockSpec((tm, D_MODEL), lambda i, f: (i, 0)),
                  pl.BlockSpec((D_MODEL, tf), lambda i, f: (0, f)),
                  pl.BlockSpec((tf, D_MODEL), lambda i, f: (f, 0)),
                  pl.BlockSpec((1, D_MODEL), lambda i, f: (0, 0)),
                  pl.BlockSpec((1, D_MODEL), lambda i, f: (0, 0))],
        out_specs=pl.BlockSpec((tm, D_MODEL), lambda i, f: (i, 0)),
        out_shape=jax.ShapeDtypeStruct((n, D_MODEL), F32),
        scratch_shapes=[pltpu.VMEM((tm, D_MODEL), BF16)],
        compiler_params=_params("parallel", "arbitrary"),
        name="mlp_ln",
    )(x1, w_up.astype(BF16), w_down.astype(BF16), ln_g.reshape(1, D_MODEL), ln_b.reshape(1, D_MODEL))


def kernel(x, w_in, w_gk2, b_gk, gla_norm_w, swa_sinks, w_out, ln1_g, ln1_b, w_up, w_down, ln2_g, ln2_b):
    bsz, seq, _ = x.shape
    n = bsz * seq
    x2 = x.reshape(n, D_MODEL)
    for l in range(DEPTH):
        qg, kg, vg, gg, gk, qs, ks, vs = _in_proj(x2, w_in[l], w_gk2[l], b_gk[l])
        to3 = lambda t: t.reshape(bsz, seq, t.shape[-1])
        gla = _gla(to3(qg), to3(kg), to3(vg), to3(gk), to3(gg), gla_norm_w[l])
        swa = _swa(to3(qs), to3(ks), to3(vs), swa_sinks[l])
        x1 = _out_proj(gla.reshape(n, D_GLA), swa.reshape(n, D_SWA), x2, w_out[l], ln1_g[l], ln1_b[l])
        x2 = _mlp(x1, w_up[l], w_down[l], ln2_g[l], ln2_b[l])
    return x2.reshape(bsz, seq, D_MODEL)
```

```python
import functools

import jax
import jax.numpy as jnp
from jax import lax
from jax.experimental import pallas as pl
from jax.experimental.pallas import tpu as pltpu

F32 = jnp.float32
BF16 = jnp.bfloat16

D_MODEL = 2048
D_GLA = 1024
GLA_HEADS = 4
GLA_DK = 128
GLA_DV = 256
GLA_KW = GLA_HEADS * GLA_DK
GATE_RANK = 16
GATE_TAU = 16.0
SWA_HEAD_DIM = 64
SWA_Q_HEADS = 16
SWA_KV_HEADS = 2
SWA_GROUP = SWA_Q_HEADS // SWA_KV_HEADS
D_SWA = SWA_Q_HEADS * SWA_HEAD_DIM
SWA_KVW = SWA_KV_HEADS * SWA_HEAD_DIM
WINDOW = 128
D_FF = 4 * D_MODEL
SPLITS = (GLA_KW, GLA_KW, D_GLA, D_GLA, GATE_RANK, D_SWA, SWA_KVW, SWA_KVW)
DEPTH = 1
ALPHA = (2 * DEPTH) ** 0.25
LN_EPS = 1e-5
RMS_EPS = 1e-5

LANES = 128
SUBLANES = 8
VMEM_LIMIT_BYTES = 56 * 1024 * 1024

PROJ_ROWS = 512
LN_SUB_ROWS = 128
GLA_CHUNK = 128
GLA_STEP_ROWS = 512
MLP_ROWS = 512
MLP_FF_TILE = 1024
NEG_BIG = -1e30


def _nn(a, b):
    return lax.dot_general(a, b, (((1,), (0,)), ((), ())), preferred_element_type=F32)


def _nt(a, b):
    return lax.dot_general(a, b, (((1,), (1,)), ((), ())), preferred_element_type=F32)


def _tn(a, b):
    return lax.dot_general(a, b, (((0,), (0,)), ((), ())), preferred_element_type=F32)


def _layer_norm(y, g, b):
    mu = jnp.mean(y, axis=-1, keepdims=True)
    d = y - mu
    var = jnp.mean(d * d, axis=-1, keepdims=True)
    return d * lax.rsqrt(var + LN_EPS) * g + b


def _params(*semantics):
    return pltpu.CompilerParams(dimension_semantics=semantics, vmem_limit_bytes=VMEM_LIMIT_BYTES)


def _resident(shape):
    nd = len(shape)
    return pl.BlockSpec(shape, lambda *_: (0,) * nd, pipeline_mode=pl.Buffered(1))


def _in_proj_kernel(x_ref, wqg, wkg, wvg, wgg, wgl, wqs, wkvs, wgk2, bgk,
                    qg_o, kg_o, vg_o, gg_o, gk_o, qs_o, kvs_o):
    xb = x_ref[...].astype(BF16)
    for w, o in ((wqg, qg_o), (wkg, kg_o), (wvg, vg_o), (wgg, gg_o), (wqs, qs_o), (wkvs, kvs_o)):
        o[...] = _nn(xb, w[...]).astype(o.dtype)
    gl = _nn(xb, wgl[...]).astype(BF16)
    z = _nn(gl, wgk2[...]) + bgk[...]
    gk_o[...] = (jnp.minimum(z, 0.0) - jnp.log1p(jnp.exp(-jnp.abs(z)))) * (1.0 / GATE_TAU)


def _in_proj(x2, w_in, w_gk2, b_gk):
    n = x2.shape[0]
    tm = min(PROJ_ROWS, n)
    bounds = [0]
    for s in SPLITS:
        bounds.append(bounds[-1] + s)
    cols = [w_in[:, bounds[i]:bounds[i + 1]].astype(BF16) for i in range(len(SPLITS))]
    wqg, wkg, wvg, wgg, wgl, wqs, wks, wvs = cols
    wkvs = jnp.concatenate([wks, wvs], axis=1)
    wgl = jnp.pad(wgl, ((0, 0), (0, LANES - GATE_RANK)))
    wgk2 = jnp.pad(w_gk2.astype(BF16), ((0, LANES - GATE_RANK), (0, 0)))
    bgk = b_gk.reshape(1, GLA_KW).astype(F32)
    weights = (wqg, wkg, wvg, wgg, wgl, wqs, wkvs, wgk2, bgk)
    out_dims = (GLA_KW, GLA_KW, D_GLA, D_GLA, GLA_KW, D_SWA, 2 * SWA_KVW)
    out_dtypes = (BF16, BF16, BF16, BF16, F32, BF16, BF16)
    return pl.pallas_call(
        _in_proj_kernel,
        grid=(n // tm,),
        in_specs=[pl.BlockSpec((tm, D_MODEL), lambda i: (i, 0))] + [_resident(w.shape) for w in weights],
        out_specs=[pl.BlockSpec((tm, d), lambda i: (i, 0)) for d in out_dims],
        out_shape=[jax.ShapeDtypeStruct((n, d), t) for d, t in zip(out_dims, out_dtypes)],
        compiler_params=_params("parallel"),
        name="in_proj",
    )(x2, *weights)


def _gla_kernel(q_ref, k_ref, v_ref, gk_ref, g_ref, nw_ref, o_ref, st_ref, b_ref, *, chunk, step_rows):
    c = chunk

    @pl.when(pl.program_id(2) == 0)
    def _():
        st_ref[...] = jnp.zeros_like(st_ref)

    row_cc = lax.broadcasted_iota(jnp.int32, (c, c), 0)
    col_cc = lax.broadcasted_iota(jnp.int32, (c, c), 1)
    tril = (col_cc <= row_cc).astype(BF16)
    eye = row_cc == col_cc
    blk_xor = row_cc ^ col_cc
    row_cd = lax.broadcasted_iota(jnp.int32, (c, GLA_DK), 0)
    sub8 = lax.broadcasted_iota(jnp.int32, (SUBLANES, GLA_DK), 0)
    scale = GLA_DK ** -0.5
    nw = nw_ref[...]

    for ci in range(step_rows // c):
        rows = pl.ds(ci * c, c)
        gk = gk_ref[rows, :]
        gk_hi = gk.astype(BF16)
        gk_lo = (gk - gk_hi.astype(F32)).astype(BF16)
        b = _nn(tril, gk_hi) + _nn(tril, gk_lo)
        b_ref[...] = b
        b_last = b_ref[pl.ds(c - 1, 1), :]
        q = q_ref[rows, :].astype(F32) * scale
        k = k_ref[rows, :].astype(F32)
        v = v_ref[rows, :]

        attn = jnp.where(eye, jnp.sum(q * k, axis=-1, keepdims=True), 0.0)
        s = c // 2
        while s >= 1:
            blk = 2 * s
            second = (row_cd & s) != 0
            if s == 1:
                expo = jnp.where(second, gk, 0.0)
            else:
                if blk >= SUBLANES:
                    pieces = [jnp.broadcast_to(b_ref[pl.ds(r0 + s - 1, 1), :], (blk, GLA_DK))
                              for r0 in range(0, c, blk)]
                else:
                    pieces = []
                    for r0 in range(0, c, SUBLANES):
                        cand = [jnp.broadcast_to(b_ref[pl.ds(r0 + o + s - 1, 1), :], (SUBLANES, GLA_DK))
                                for o in range(0, SUBLANES, blk)]
                        m8 = cand[-1]
                        for idx in range(len(cand) - 2, -1, -1):
                            m8 = jnp.where(sub8 < (idx + 1) * blk, cand[idx], m8)
                        pieces.append(m8)
                mid = pieces[0] if len(pieces) == 1 else jnp.concatenate(pieces, axis=0)
                expo = jnp.where(second, b - mid, mid - b)
            f = jnp.exp(expo)
            qs = jnp.where(second, q * f, 0.0).astype(BF16)
            ks = jnp.where(second, 0.0, k * f).astype(BF16)
            r = _nt(qs, ks)
            attn = attn + (r if blk == c else jnp.where(blk_xor < blk, r, 0.0))
            s //= 2

        st = st_ref[...]
        q_in = (q * jnp.exp(b)).astype(BF16)
        o = _nt(q_in, st.astype(BF16)) + _nn(attn.astype(BF16), v)
        k_out = (k * jnp.exp(b_last - b)).astype(BF16)
        st_ref[...] = st * jnp.exp(b_last) + _tn(v, k_out)

        o = o * lax.rsqrt(jnp.mean(o * o, axis=-1, keepdims=True) + RMS_EPS)
        g = g_ref[rows, :].astype(F32)
        o = o * nw * (g * (1.0 / (1.0 + jnp.exp(-g))))
        o_ref[rows, :] = o.astype(o_ref.dtype)


def _gla(qg, kg, vg, gk, gg, norm_w):
    bsz, seq, _ = qg.shape
    step = min(GLA_STEP_ROWS, seq)
    chunk = min(GLA_CHUNK, step)
    kern = functools.partial(_gla_kernel, chunk=chunk, step_rows=step)
    kspec = pl.BlockSpec((None, step, GLA_DK), lambda b, h, t: (b, t, h))
    vspec = pl.BlockSpec((None, step, GLA_DV), lambda b, h, t: (b, t, h))
    return pl.pallas_call(
        kern,
        grid=(bsz, GLA_HEADS, seq // step),
        in_specs=[kspec, kspec, vspec, kspec, vspec, pl.BlockSpec((1, GLA_DV), lambda b, h, t: (0, 0))],
        out_specs=vspec,
        out_shape=jax.ShapeDtypeStruct((bsz, seq, D_GLA), BF16),
        scratch_shapes=[pltpu.VMEM((GLA_DV, GLA_DK), F32), pltpu.VMEM((chunk, GLA_DK), F32)],
        compiler_params=_params("parallel", "parallel", "arbitrary"),
        name="gla_mixer",
    )(qg, kg, vg, gk, gg, norm_w.reshape(1, GLA_DV).astype(F32))


def _swa_kernel(sink_ref, q_ref, kp_ref, kc_ref, vp_ref, vc_ref, o_ref):
    w = WINDOW
    hd = SWA_HEAD_DIM
    n = pl.program_id(1)
    kf = jnp.concatenate([kp_ref[...], kc_ref[...]], axis=0).astype(F32)
    vf = jnp.concatenate([vp_ref[...], vc_ref[...]], axis=0).astype(F32)
    lane_kv = lax.broadcasted_iota(jnp.int32, (2 * w, LANES), 1)
    qi = lax.broadcasted_iota(jnp.int32, (w, 2 * w), 0)
    kj = lax.broadcasted_iota(jnp.int32, (w, 2 * w), 1)
    first_key = jnp.where(n > 0, 0, w)
    valid = (kj > qi) & (kj <= qi + w) & (kj >= first_key)
    lane_o = lax.broadcasted_iota(jnp.int32, (w, LANES), 1)
    pairs = SWA_GROUP // 2
    for g in range(SWA_KV_HEADS):
        k_rot = kf if g == 0 else pltpu.roll(kf, hd, axis=1)
        v_rot = vf if g == 0 else pltpu.roll(vf, hd, axis=1)
        k_lo = jnp.where(lane_kv < hd, k_rot, 0.0)
        v_lo = jnp.where(lane_kv < hd, v_rot, 0.0)
        k_hi = pltpu.roll(k_lo, hd, axis=1).astype(BF16)
        v_hi = pltpu.roll(v_lo, hd, axis=1).astype(BF16)
        k_lo = k_lo.astype(BF16)
        v_lo = v_lo.astype(BF16)
        qst = jnp.concatenate(
            [q_ref[:, pl.ds((g * pairs + p) * LANES, LANES)] for p in range(pairs)], axis=0)
        s_lo = _nt(qst, k_lo)
        s_hi = _nt(qst, k_hi)
        p_parts, inv_parts = [], []
        for s_all, par in ((s_lo, 0), (s_hi, 1)):
            ps, invs = [], []
            for p in range(pairs):
                sink = sink_ref[2 * (g * pairs + p) + par]
                s = jnp.where(valid, s_all[p * w:(p + 1) * w, :] * (hd ** -0.5), NEG_BIG)
                m = jnp.maximum(jnp.max(s, axis=-1, keepdims=True), sink)
                e = jnp.exp(s - m)
                denom = jnp.sum(e, axis=-1, keepdims=True) + jnp.exp(sink - m)
                ps.append(e.astype(BF16))
                invs.append(1.0 / denom)
            p_parts.append(jnp.concatenate(ps, axis=0))
            inv_parts.append(invs)
        acc = _nn(p_parts[0], v_lo) + _nn(p_parts[1], v_hi)
        for p in range(pairs):
            inv = jnp.where(lane_o < hd, inv_parts[0][p], inv_parts[1][p])
            o_ref[:, pl.ds((g * pairs + p) * LANES, LANES)] = (acc[p * w:(p + 1) * w, :] * inv).astype(o_ref.dtype)


def _swa(qs, kvs, sinks):
    bsz, seq, _ = qs.shape
    w = WINDOW
    cur = lambda b, n: (b, n, 0)
    kv = lambda col, back: pl.BlockSpec((None, w, SWA_KVW), lambda b, n: (b, jnp.maximum(n - back, 0), col))
    return pl.pallas_call(
        _swa_kernel,
        grid=(bsz, seq // w),
        in_specs=[pl.BlockSpec(memory_space=pltpu.SMEM),
                  pl.BlockSpec((None, w, D_SWA), cur), kv(0, 1), kv(0, 0), kv(1, 1), kv(1, 0)],
        out_specs=pl.BlockSpec((None, w, D_SWA), cur),
        out_shape=jax.ShapeDtypeStruct((bsz, seq, D_SWA), BF16),
        compiler_params=_params("parallel", "arbitrary"),
        name="swa_mixer",
    )(sinks.astype(F32), qs, kvs, kvs, kvs, kvs)


def _out_proj_kernel(gla_ref, swa_ref, x_ref, wa_ref, wb_ref, g_ref, b_ref, o_ref):
    for r0 in range(0, x_ref.shape[0], LN_SUB_ROWS):
        rows = pl.ds(r0, LN_SUB_ROWS)
        mix = _nn(gla_ref[rows, :], wa_ref[...]) + _nn(swa_ref[rows, :], wb_ref[...])
        o_ref[rows, :] = _layer_norm(ALPHA * x_ref[rows, :] + mix, g_ref[...], b_ref[...])


def _out_proj(gla, swa, x2, w_out, ln_g, ln_b):
    n = x2.shape[0]
    tm = min(PROJ_ROWS, n)
    wa = w_out[:D_GLA].astype(BF16)
    wb = w_out[D_GLA:].astype(BF16)
    row = lambda d: pl.BlockSpec((tm, d), lambda i: (i, 0))
    return pl.pallas_call(
        _out_proj_kernel,
        grid=(n // tm,),
        in_specs=[row(D_GLA), row(D_SWA), row(D_MODEL), _resident(wa.shape), _resident(wb.shape),
                  _resident((1, D_MODEL)), _resident((1, D_MODEL))],
        out_specs=row(D_MODEL),
        out_shape=jax.ShapeDtypeStruct((n, D_MODEL), F32),
        compiler_params=_params("parallel"),
        name="out_proj_ln",
    )(gla, swa, x2, wa, wb, ln_g.reshape(1, D_MODEL), ln_b.reshape(1, D_MODEL))


def _mlp_kernel(x_ref, wu_ref, wd_ref, g_ref, b_ref, o_ref, xb_ref):
    f = pl.program_id(1)

    @pl.when(f == 0)
    def _():
        x = x_ref[...]
        xb_ref[...] = x.astype(BF16)
        o_ref[...] = ALPHA * x

    h = jnp.maximum(_nn(xb_ref[...], wu_ref[...]), 0.0)
    o_ref[...] += _nn((h * h).astype(BF16), wd_ref[...])

    @pl.when(f == pl.num_programs(1) - 1)
    def _():
        o_ref[...] = _layer_norm(o_ref[...], g_ref[...], b_ref[...])


def _mlp(x1, w_up, w_down, ln_g, ln_b):
    n = x1.shape[0]
    tm = min(MLP_ROWS, n)
    tf = MLP_FF_TILE
    return pl.pallas_call(
        _mlp_kernel,
        grid=(n // tm, D_FF // tf),
        in_specs=[pl.BlockSpec((tm, D_MODEL), lambda i, f: (i, 0)),
                  pl.BlockSpec((D_MODEL, tf), lambda i, f: (0, f)),
                  pl.BlockSpec((tf, D_MODEL), lambda i, f: (f, 0)),
                  pl.BlockSpec((1, D_MODEL), lambda i, f: (0, 0)),
                  pl.BlockSpec((1, D_MODEL), lambda i, f: (0, 0))],
        out_specs=pl.BlockSpec((tm, D_MODEL), lambda i, f: (i, 0)),
        out_shape=jax.ShapeDtypeStruct((n, D_MODEL), F32),
        scratch_shapes=[pltpu.VMEM((tm, D_MODEL), BF16)],
        compiler_params=_params("parallel", "arbitrary"),
        name="mlp_ln",
    )(x1, w_up.astype(BF16), w_down.astype(BF16), ln_g.reshape(1, D_MODEL), ln_b.reshape(1, D_MODEL))


def kernel(x, w_in, w_gk2, b_gk, gla_norm_w, swa_sinks, w_out, ln1_g, ln1_b, w_up, w_down, ln2_g, ln2_b):
    bsz, seq, _ = x.shape
    n = bsz * seq
    x2 = x.reshape(n, D_MODEL)
    for l in range(DEPTH):
        qg, kg, vg, gg, gk, qs, kvs = _in_proj(x2, w_in[l], w_gk2[l], b_gk[l])
        to3 = lambda t: t.reshape(bsz, seq, t.shape[-1])
        gla = _gla(to3(qg), to3(kg), to3(vg), to3(gk), to3(gg), gla_norm_w[l])
        swa = _swa(to3(qs), to3(kvs), swa_sinks[l])
        x1 = _out_proj(gla.reshape(n, D_GLA), swa.reshape(n, D_SWA), x2, w_out[l], ln1_g[l], ln1_b[l])
        x2 = _mlp(x1, w_up[l], w_down[l], ln2_g[l], ln2_b[l])
    return x2.reshape(bsz, seq, D_MODEL)
```

```python
import functools

import jax
import jax.numpy as jnp
from jax import lax
from jax.experimental import pallas as pl
from jax.experimental.pallas import tpu as pltpu

F32 = jnp.float32
BF16 = jnp.bfloat16

D_MODEL = 2048
D_GLA = 1024
GLA_HEADS = 4
GLA_DK = 128
GLA_DV = 256
GLA_KW = GLA_HEADS * GLA_DK
GATE_RANK = 16
GATE_TAU = 16.0
SWA_HEAD_DIM = 64
SWA_Q_HEADS = 16
SWA_KV_HEADS = 2
SWA_GROUP = SWA_Q_HEADS // SWA_KV_HEADS
D_SWA = SWA_Q_HEADS * SWA_HEAD_DIM
SWA_KVW = SWA_KV_HEADS * SWA_HEAD_DIM
WINDOW = 128
D_FF = 4 * D_MODEL
SPLITS = (GLA_KW, GLA_KW, D_GLA, D_GLA, GATE_RANK, D_SWA, SWA_KVW, SWA_KVW)
DEPTH = 1
ALPHA = (2 * DEPTH) ** 0.25
LN_EPS = 1e-5
RMS_EPS = 1e-5

LANES = 128
SUBLANES = 8
VMEM_LIMIT_BYTES = 56 * 1024 * 1024

PROJ_ROWS = 512
LN_SUB_ROWS = 128
GLA_CHUNK = 128
GLA_STEP_ROWS = 512
MLP_ROWS = 512
MLP_FF_TILE = 1024
NEG_BIG = -1e30
LOG2_E = 1.4426950408889634

assert PROJ_ROWS % GLA_CHUNK == 0 and GLA_STEP_ROWS % GLA_CHUNK == 0


def _nn(a, b):
    return lax.dot_general(a, b, (((1,), (0,)), ((), ())), preferred_element_type=F32)


def _nt(a, b):
    return lax.dot_general(a, b, (((1,), (1,)), ((), ())), preferred_element_type=F32)


def _tn(a, b):
    return lax.dot_general(a, b, (((0,), (0,)), ((), ())), preferred_element_type=F32)


def _layer_norm(y, g, b):
    mu = jnp.mean(y, axis=-1, keepdims=True)
    d = y - mu
    var = jnp.mean(d * d, axis=-1, keepdims=True)
    return d * lax.rsqrt(var + LN_EPS) * g + b


def _params(*semantics):
    return pltpu.CompilerParams(dimension_semantics=semantics, vmem_limit_bytes=VMEM_LIMIT_BYTES)


def _resident(shape):
    nd = len(shape)
    return pl.BlockSpec(shape, lambda *_: (0,) * nd, pipeline_mode=pl.Buffered(1))


def _in_proj_kernel(x_ref, wqg, wkg, wvg, wgg, wgl, wqs, wkvs, wgk2, bgk, nw,
                    qg_o, kg_o, vg_o, gate_o, gk_o, b_o, qs_o, kvs_o, *, chunk):
    xb = x_ref[...].astype(BF16)
    qg_o[...] = (_nn(xb, wqg[...]) * (GLA_DK ** -0.5)).astype(BF16)
    qs_o[...] = (_nn(xb, wqs[...]) * (SWA_HEAD_DIM ** -0.5)).astype(BF16)
    for w, o in ((wkg, kg_o), (wvg, vg_o), (wkvs, kvs_o)):
        o[...] = _nn(xb, w[...]).astype(BF16)
    g = _nn(xb, wgg[...])
    gate_o[...] = (nw[...] * g / (1.0 + jnp.exp(-g))).astype(BF16)
    gl = _nn(xb, wgl[...]).astype(BF16)
    z = _nn(gl, wgk2[...]) + bgk[...]
    gk = (jnp.minimum(z, 0.0) - jnp.log1p(jnp.exp(-jnp.abs(z)))) * (LOG2_E / GATE_TAU)
    gk_o[...] = gk
    gk_hi = gk.astype(BF16)
    gk_lo = (gk - gk_hi.astype(F32)).astype(BF16)
    row = lax.broadcasted_iota(jnp.int32, (chunk, chunk), 0)
    col = lax.broadcasted_iota(jnp.int32, (chunk, chunk), 1)
    tril = (col <= row).astype(BF16)
    for r0 in range(0, x_ref.shape[0], chunk):
        b_o[r0:r0 + chunk, :] = _nn(tril, gk_hi[r0:r0 + chunk, :]) + _nn(tril, gk_lo[r0:r0 + chunk, :])


def _in_proj(x2, w_in, w_gk2, b_gk, norm_w):
    n = x2.shape[0]
    tm = min(PROJ_ROWS, n)
    chunk = min(GLA_CHUNK, tm)
    bounds = [0]
    for s in SPLITS:
        bounds.append(bounds[-1] + s)
    cols = [w_in[:, bounds[i]:bounds[i + 1]].astype(BF16) for i in range(len(SPLITS))]
    wqg, wkg, wvg, wgg, wgl, wqs, wks, wvs = cols
    wkvs = jnp.concatenate([wks, wvs], axis=1)
    wgl = jnp.pad(wgl, ((0, 0), (0, LANES - GATE_RANK)))
    wgk2 = jnp.pad(w_gk2.astype(BF16), ((0, LANES - GATE_RANK), (0, 0)))
    bgk = b_gk.reshape(1, GLA_KW).astype(F32)
    nw = jnp.tile(norm_w.astype(F32), GLA_HEADS).reshape(1, D_GLA)
    weights = (wqg, wkg, wvg, wgg, wgl, wqs, wkvs, wgk2, bgk, nw)
    out_dims = (GLA_KW, GLA_KW, D_GLA, D_GLA, GLA_KW, GLA_KW, D_SWA, 2 * SWA_KVW)
    out_dtypes = (BF16, BF16, BF16, BF16, F32, F32, BF16, BF16)
    return pl.pallas_call(
        functools.partial(_in_proj_kernel, chunk=chunk),
        grid=(n // tm,),
        in_specs=[pl.BlockSpec((tm, D_MODEL), lambda i: (i, 0))] + [_resident(w.shape) for w in weights],
        out_specs=[pl.BlockSpec((tm, d), lambda i: (i, 0)) for d in out_dims],
        out_shape=[jax.ShapeDtypeStruct((n, d), t) for d, t in zip(out_dims, out_dtypes)],
        compiler_params=_params("parallel"),
        name="in_proj",
    )(x2, *weights)


def _gla_kernel(q_ref, k_ref, v_ref, b_ref, gk_ref, gate_ref, o_ref, st_ref, *, chunk, step_rows):
    c = chunk
    dk = GLA_DK
    dv = GLA_DV

    @pl.when(pl.program_id(1) == 0)
    def _():
        st_ref[...] = jnp.zeros_like(st_ref)

    row_cc = lax.broadcasted_iota(jnp.int32, (c, c), 0)
    col_cc = lax.broadcasted_iota(jnp.int32, (c, c), 1)
    eye = row_cc == col_cc
    small = [s for s in (1, 2, 4) if 2 * s <= min(c, SUBLANES)]
    small_mask = {s: ((row_cc & s) != 0) & ((col_cc & s) == 0) & ((row_cc ^ col_cc) < 2 * s) for s in small}
    row_cd = lax.broadcasted_iota(jnp.int32, (c, dk), 0)
    sub8 = lax.broadcasted_iota(jnp.int32, (SUBLANES, dk), 0)
    lane8 = lax.broadcasted_iota(jnp.int32, (SUBLANES, c), 1)

    for ci in range(step_rows // c):
        base = ci * c
        rows = pl.ds(base, c)
        for h in range(GLA_HEADS):
            kcols = pl.ds(h * dk, dk)
            vcols = pl.ds(h * dv, dv)

            def brow(r):
                return jnp.broadcast_to(b_ref[pl.ds(r, 1), kcols], (SUBLANES, dk))

            b = b_ref[rows, kcols]
            qb = q_ref[rows, kcols]
            kb = k_ref[rows, kcols]
            q = qb.astype(F32)
            k = kb.astype(F32)
            v = v_ref[rows, vcols]
            b_last = b_ref[pl.ds(base + c - 1, 1), kcols]

            attn = jnp.where(eye, _nt(qb, kb), 0.0)
            for s in small:
                upper = (row_cd & s) != 0
                if s == 1:
                    expo = jnp.where(upper, gk_ref[rows, kcols], 0.0)
                else:
                    tiles = []
                    for r0 in range(0, c, SUBLANES):
                        mids = [brow(base + r0 + o + s - 1) for o in range(0, SUBLANES, 2 * s)]
                        m8 = mids[-1]
                        for idx in range(len(mids) - 2, -1, -1):
                            m8 = jnp.where(sub8 < (idx + 1) * 2 * s, mids[idx], m8)
                        tiles.append(m8)
                    expo = (b - jnp.concatenate(tiles, axis=0)) * jnp.where(upper, 1.0, -1.0)
                t = (jnp.where(upper, q, k) * jnp.exp2(expo)).astype(BF16)
                attn = jnp.where(small_mask[s], _nt(t, t), attn)

            slabs = [attn[r:r + SUBLANES, :] for r in range(0, c, SUBLANES)]
            s = SUBLANES
            while 2 * s <= c:
                sel, expo = [], []
                for r0 in range(0, c, 2 * s):
                    mid = jnp.concatenate([brow(base + r0 + s - 1)] * (s // SUBLANES), axis=0)
                    sel += [k[r0:r0 + s, :], q[r0 + s:r0 + 2 * s, :]]
                    expo += [mid - b[r0:r0 + s, :], b[r0 + s:r0 + 2 * s, :] - mid]
                t = (jnp.concatenate(sel, axis=0) * jnp.exp2(jnp.concatenate(expo, axis=0))).astype(BF16)
                g_lvl = _nt(t, t)
                for r0 in range(0, c, 2 * s):
                    in_lower = (lane8 >= r0) & (lane8 < r0 + s)
                    for r in range(r0 + s, r0 + 2 * s, SUBLANES):
                        slabs[r // SUBLANES] = jnp.where(in_lower, g_lvl[r:r + SUBLANES, :], slabs[r // SUBLANES])
                s *= 2
            attn = jnp.concatenate(slabs, axis=0).astype(BF16)

            st = st_ref[h]
            q_in = (q * jnp.exp2(b)).astype(BF16)
            o = _nt(q_in, st.astype(BF16)) + _nn(attn, v)
            k_out = (k * jnp.exp2(b_last - b)).astype(BF16)
            st_ref[h] = st * jnp.exp2(b_last) + _tn(v, k_out)

            o = o * lax.rsqrt(jnp.mean(o * o, axis=-1, keepdims=True) + RMS_EPS)
            o_ref[rows, vcols] = (o * gate_ref[rows, vcols].astype(F32)).astype(o_ref.dtype)


def _gla(qg, kg, vg, b2, gk2, gate):
    bsz, seq, _ = qg.shape
    step = min(GLA_STEP_ROWS, seq)
    chunk = min(GLA_CHUNK, step)
    kern = functools.partial(_gla_kernel, chunk=chunk, step_rows=step)
    kspec = pl.BlockSpec((None, step, GLA_KW), lambda bi, t: (bi, t, 0))
    vspec = pl.BlockSpec((None, step, D_GLA), lambda bi, t: (bi, t, 0))
    return pl.pallas_call(
        kern,
        grid=(bsz, seq // step),
        in_specs=[kspec, kspec, vspec, kspec, kspec, vspec],
        out_specs=vspec,
        out_shape=jax.ShapeDtypeStruct((bsz, seq, D_GLA), BF16),
        scratch_shapes=[pltpu.VMEM((GLA_HEADS, GLA_DV, GLA_DK), F32)],
        compiler_params=_params("parallel", "arbitrary"),
        name="gla_mixer",
    )(qg, kg, vg, b2, gk2, gate)


def _swa_kernel(sink_ref, q_ref, kp_ref, kc_ref, vp_ref, vc_ref, o_ref):
    w = WINDOW
    hd = SWA_HEAD_DIM
    n = pl.program_id(1)
    kf = jnp.concatenate([kp_ref[...], kc_ref[...]], axis=0).astype(F32)
    vf = jnp.concatenate([vp_ref[...], vc_ref[...]], axis=0).astype(F32)
    lane_kv = lax.broadcasted_iota(jnp.int32, (2 * w, LANES), 1)
    low_kv = lane_kv < hd
    qi = lax.broadcasted_iota(jnp.int32, (w, w), 0)
    kj = lax.broadcasted_iota(jnp.int32, (w, w), 1)
    from_prev = kj > qi
    prev_penalty = jnp.where(n > 0, 0.0, NEG_BIG)
    low_o = lax.broadcasted_iota(jnp.int32, (w, LANES), 1) < hd
    ones_lo = low_kv.astype(BF16)
    ones_hi = (~low_kv).astype(BF16)
    pairs = SWA_GROUP // 2
    for g in range(SWA_KV_HEADS):
        k_rot = kf if g == 0 else pltpu.roll(kf, hd, axis=1)
        v_rot = vf if g == 0 else pltpu.roll(vf, hd, axis=1)
        k_lo = jnp.where(low_kv, k_rot, 0.0)
        v_lo = jnp.where(low_kv, v_rot, 0.0)
        k_hi = pltpu.roll(k_lo, hd, axis=1).astype(BF16)
        v_hi = pltpu.roll(v_lo, hd, axis=1).astype(BF16)
        vw = (jnp.concatenate([v_lo.astype(BF16), ones_lo], axis=1),
              jnp.concatenate([v_hi, ones_hi], axis=1))
        kk = jnp.concatenate([k_lo.astype(BF16), k_hi], axis=0)
        qst = jnp.concatenate(
            [q_ref[:, pl.ds((g * pairs + p) * LANES, LANES)] for p in range(pairs)], axis=0)
        s_all = _nt(qst, kk)
        acc = None
        sink_terms = []
        for par in range(2):
            probs, terms = [], []
            for p in range(pairs):
                sink = sink_ref[2 * (g * pairs + p) + par]
                s2 = s_all[p * w:(p + 1) * w, 2 * w * par:2 * w * (par + 1)]
                s = jnp.where(from_prev, s2[:, :w] + prev_penalty, s2[:, w:])
                m = jnp.maximum(jnp.max(s, axis=-1, keepdims=True), sink)
                e = jnp.exp(s - m)
                probs.append(jnp.concatenate([jnp.where(from_prev, e, 0.0).astype(BF16),
                                              jnp.where(from_prev, 0.0, e).astype(BF16)], axis=1))
                terms.append(jnp.exp(sink - m))
            part = _nn(jnp.concatenate(probs, axis=0), vw[par])
            acc = part if acc is None else acc + part
            sink_terms.append(terms)
        for p in range(pairs):
            blk = acc[p * w:(p + 1) * w, :]
            denom = blk[:, LANES:] + jnp.where(low_o, sink_terms[0][p], sink_terms[1][p])
            o_ref[:, pl.ds((g * pairs + p) * LANES, LANES)] = (blk[:, :LANES] / denom).astype(o_ref.dtype)


def _swa(qs, kvs, sinks):
    bsz, seq, _ = qs.shape
    w = WINDOW
    cur = lambda b, n: (b, n, 0)
    kv = lambda col, back: pl.BlockSpec((None, w, SWA_KVW), lambda b, n: (b, jnp.maximum(n - back, 0), col))
    return pl.pallas_call(
        _swa_kernel,
        grid=(bsz, seq // w),
        in_specs=[pl.BlockSpec(memory_space=pltpu.SMEM),
                  pl.BlockSpec((None, w, D_SWA), cur), kv(0, 1), kv(0, 0), kv(1, 1), kv(1, 0)],
        out_specs=pl.BlockSpec((None, w, D_SWA), cur),
        out_shape=jax.ShapeDtypeStruct((bsz, seq, D_SWA), BF16),
        compiler_params=_params("parallel", "arbitrary"),
        name="swa_mixer",
    )(sinks.astype(F32), qs, kvs, kvs, kvs, kvs)


def _out_proj_kernel(gla_ref, swa_ref, x_ref, wa_ref, wb_ref, g_ref, b_ref, o_ref):
    for r0 in range(0, x_ref.shape[0], LN_SUB_ROWS):
        rows = pl.ds(r0, LN_SUB_ROWS)
        mix = _nn(gla_ref[rows, :], wa_ref[...]) + _nn(swa_ref[rows, :], wb_ref[...])
        o_ref[rows, :] = _layer_norm(ALPHA * x_ref[rows, :] + mix, g_ref[...], b_ref[...])


def _out_proj(gla, swa, x2, w_out, ln_g, ln_b):
    n = x2.shape[0]
    tm = min(PROJ_ROWS, n)
    wa = w_out[:D_GLA].astype(BF16)
    wb = w_out[D_GLA:].astype(BF16)
    row = lambda d: pl.BlockSpec((tm, d), lambda i: (i, 0))
    return pl.pallas_call(
        _out_proj_kernel,
        grid=(n // tm,),
        in_specs=[row(D_GLA), row(D_SWA), row(D_MODEL), _resident(wa.shape), _resident(wb.shape),
                  _resident((1, D_MODEL)), _resident((1, D_MODEL))],
        out_specs=row(D_MODEL),
        out_shape=jax.ShapeDtypeStruct((n, D_MODEL), F32),
        compiler_params=_params("parallel"),
        name="out_proj_ln",
    )(gla, swa, x2, wa, wb, ln_g.reshape(1, D_MODEL), ln_b.reshape(1, D_MODEL))


def _mlp_kernel(x_ref, wu_ref, wd_ref, g_ref, b_ref, o_ref, xb_ref):
    f = pl.program_id(1)

    @pl.when(f == 0)
    def _():
        x = x_ref[...]
        xb_ref[...] = x.astype(BF16)
        o_ref[...] = ALPHA * x

    h = jnp.maximum(_nn(xb_ref[...], wu_ref[...]), 0.0)
    o_ref[...] += _nn((h * h).astype(BF16), wd_ref[...])

    @pl.when(f == pl.num_programs(1) - 1)
    def _():
        o_ref[...] = _layer_norm(o_ref[...], g_ref[...], b_ref[...])


def _mlp(x1, w_up, w_down, ln_g, ln_b):
    n = x1.shape[0]
    tm = min(MLP_ROWS, n)
    tf = MLP_FF_TILE
    return pl.pallas_call(
        _mlp_kernel,
        grid=(n // tm, D_FF // tf),
        in_specs=[pl.BlockSpec((tm, D_MODEL), lambda i, f: (i, 0)),
                  pl.BlockSpec((D_MODEL, tf), lambda i, f: (0, f)),
                  pl.BlockSpec((tf, D_MODEL), lambda i, f: (f, 0)),
                  pl.BlockSpec((1, D_MODEL), lambda i, f: (0, 0)),
                  pl.BlockSpec((1, D_MODEL), lambda i, f: (0, 0))],
        out_specs=pl.BlockSpec((tm, D_MODEL), lambda i, f: (i, 0)),
        out_shape=jax.ShapeDtypeStruct((n, D_MODEL), F32),
        scratch_shapes=[pltpu.VMEM((tm, D_MODEL), BF16)],
        compiler_params=_params("parallel", "arbitrary"),
        name="mlp_ln",
    )(x1, w_up.astype(BF16), w_down.astype(BF16), ln_g.reshape(1, D_MODEL), ln_b.reshape(1, D_MODEL))


def kernel(x, w_in, w_gk2, b_gk, gla_norm_w, swa_sinks, w_out, ln1_g, ln1_b, w_up, w_down, ln2_g, ln2_b):
    bsz, seq, _ = x.shape
    n = bsz * seq
    x2 = x.reshape(n, D_MODEL)
    for l in range(DEPTH):
        qg, kg, vg, gate, gk, b, qs, kvs = _in_proj(x2, w_in[l], w_gk2[l], b_gk[l], gla_norm_w[l])
        to3 = lambda t: t.reshape(bsz, seq, t.shape[-1])
        gla = _gla(to3(qg), to3(kg), to3(vg), to3(b), to3(gk), to3(gate))
        swa = _swa(to3(qs), to3(kvs), swa_sinks[l])
        x1 = _out_proj(gla.reshape(n, D_GLA), swa.reshape(n, D_SWA), x2, w_out[l], ln1_g[l], ln1_b[l])
        x2 = _mlp(x1, w_up[l], w_down[l], ln2_g[l], ln2_b[l])
    return x2.reshape(bsz, seq, D_MODEL)
```

```python
import functools

import jax
import jax.numpy as jnp
from jax import lax
from jax.experimental import pallas as pl
from jax.experimental.pallas import tpu as pltpu

F32 = jnp.float32
BF16 = jnp.bfloat16

D_MODEL = 2048
D_GLA = 1024
GLA_HEADS = 4
GLA_DK = 128
GLA_DV = 256
GLA_KW = GLA_HEADS * GLA_DK
GATE_RANK = 16
GATE_TAU = 16.0
SWA_HEAD_DIM = 64
SWA_Q_HEADS = 16
SWA_KV_HEADS = 2
SWA_GROUP = SWA_Q_HEADS // SWA_KV_HEADS
D_SWA = SWA_Q_HEADS * SWA_HEAD_DIM
SWA_KVW = SWA_KV_HEADS * SWA_HEAD_DIM
WINDOW = 128
D_FF = 4 * D_MODEL
SPLITS = (GLA_KW, GLA_KW, D_GLA, D_GLA, GATE_RANK, D_SWA, SWA_KVW, SWA_KVW)
DEPTH = 1
ALPHA = (2 * DEPTH) ** 0.25
LN_EPS = 1e-5
RMS_EPS = 1e-5

LANES = 128
SUBLANES = 8
MXU_COLS = 256
VMEM_LIMIT_BYTES = 56 * 1024 * 1024

MIX_ROWS = 512
PROJ_ROWS = 512
LN_SUB_ROWS = 128
GLA_CHUNK = 128
MLP_ROWS = 512
MLP_FF_TILE = 1024
NEG_BIG = -1e30
LOG2_E = 1.4426950408889634

assert MIX_ROWS % GLA_CHUNK == 0 and MIX_ROWS % WINDOW == 0


def _nn(a, b):
    return lax.dot_general(a, b, (((1,), (0,)), ((), ())), preferred_element_type=F32)


def _nt(a, b):
    return lax.dot_general(a, b, (((1,), (1,)), ((), ())), preferred_element_type=F32)


def _tn(a, b):
    return lax.dot_general(a, b, (((0,), (0,)), ((), ())), preferred_element_type=F32)


def _layer_norm(y, g, b):
    mu = jnp.mean(y, axis=-1, keepdims=True)
    d = y - mu
    var = jnp.mean(d * d, axis=-1, keepdims=True)
    return d * lax.rsqrt(var + LN_EPS) * g + b


def _params(*semantics):
    return pltpu.CompilerParams(dimension_semantics=semantics, vmem_limit_bytes=VMEM_LIMIT_BYTES)


def _resident(shape):
    nd = len(shape)
    return pl.BlockSpec(shape, lambda *_: (0,) * nd, pipeline_mode=pl.Buffered(1))


def _gla_chunk_head(q_ref, k_ref, v_ref, b_ref, gk_ref, gate_ref, st_ref, o_ref, slot, base, h, c, consts):
    eye, small, small_mask, row_cd, sub8, lane8 = consts
    dk, dv = GLA_DK, GLA_DV
    rows = pl.ds(base, c)
    kcols = pl.ds(h * dk, dk)
    vcols = pl.ds(h * dv, dv)

    def brow(r):
        return jnp.broadcast_to(b_ref[slot, pl.ds(r, 1), kcols], (SUBLANES, dk))

    b = b_ref[slot, rows, kcols]
    qb = q_ref[slot, rows, kcols]
    kb = k_ref[slot, rows, kcols]
    q = qb.astype(F32)
    k = kb.astype(F32)
    v = v_ref[slot, rows, vcols]
    b_last = b_ref[slot, pl.ds(base + c - 1, 1), kcols]

    attn = jnp.where(eye, _nt(qb, kb), 0.0)
    for s in small:
        upper = (row_cd & s) != 0
        if s == 1:
            expo = jnp.where(upper, gk_ref[slot, rows, kcols], 0.0)
        else:
            tiles = []
            for r0 in range(0, c, SUBLANES):
                mids = [brow(base + r0 + o + s - 1) for o in range(0, SUBLANES, 2 * s)]
                m8 = mids[-1]
                for idx in range(len(mids) - 2, -1, -1):
                    m8 = jnp.where(sub8 < (idx + 1) * 2 * s, mids[idx], m8)
                tiles.append(m8)
            expo = (b - jnp.concatenate(tiles, axis=0)) * jnp.where(upper, 1.0, -1.0)
        t = (jnp.where(upper, q, k) * jnp.exp2(expo)).astype(BF16)
        attn = jnp.where(small_mask[s], _nt(t, t), attn)

    slabs = [attn[r:r + SUBLANES, :] for r in range(0, c, SUBLANES)]
    s = SUBLANES
    while 2 * s <= c:
        sel, expo = [], []
        for r0 in range(0, c, 2 * s):
            mid = jnp.concatenate([brow(base + r0 + s - 1)] * (s // SUBLANES), axis=0)
            sel += [k[r0:r0 + s, :], q[r0 + s:r0 + 2 * s, :]]
            expo += [mid - b[r0:r0 + s, :], b[r0 + s:r0 + 2 * s, :] - mid]
        t = (jnp.concatenate(sel, axis=0) * jnp.exp2(jnp.concatenate(expo, axis=0))).astype(BF16)
        g_lvl = _nt(t, t)
        for r0 in range(0, c, 2 * s):
            in_lower = (lane8 >= r0) & (lane8 < r0 + s)
            for r in range(r0 + s, r0 + 2 * s, SUBLANES):
                slabs[r // SUBLANES] = jnp.where(in_lower, g_lvl[r:r + SUBLANES, :], slabs[r // SUBLANES])
        s *= 2
    attn = jnp.concatenate(slabs, axis=0).astype(BF16)

    st = st_ref[h]
    q_in = (q * jnp.exp2(b)).astype(BF16)
    o = _nt(q_in, st.astype(BF16)) + _nn(attn, v)
    k_out = (k * jnp.exp2(b_last - b)).astype(BF16)
    st_ref[h] = st * jnp.exp2(b_last) + _tn(v, k_out)

    o = o * lax.rsqrt(jnp.mean(o * o, axis=-1, keepdims=True) + RMS_EPS)
    o_ref[rows, vcols] = (o * gate_ref[slot, rows, vcols].astype(F32)).astype(o_ref.dtype)


def _gla_consts(c):
    row_cc = lax.broadcasted_iota(jnp.int32, (c, c), 0)
    col_cc = lax.broadcasted_iota(jnp.int32, (c, c), 1)
    eye = row_cc == col_cc
    small = [s for s in (1, 2, 4) if 2 * s <= min(c, SUBLANES)]
    small_mask = {s: ((row_cc & s) != 0) & ((col_cc & s) == 0) & ((row_cc ^ col_cc) < 2 * s) for s in small}
    row_cd = lax.broadcasted_iota(jnp.int32, (c, GLA_DK), 0)
    sub8 = lax.broadcasted_iota(jnp.int32, (SUBLANES, GLA_DK), 0)
    lane8 = lax.broadcasted_iota(jnp.int32, (SUBLANES, c), 1)
    return eye, small, small_mask, row_cd, sub8, lane8


def _swa_block(sink_ref, q_ref, slot, row0, k_prev, k_cur, v_prev, v_cur, prev_penalty, o_ref):
    w = WINDOW
    hd = SWA_HEAD_DIM
    rows = pl.ds(row0, w)
    kf = jnp.concatenate([k_prev, k_cur], axis=0).astype(F32)
    vf = jnp.concatenate([v_prev, v_cur], axis=0).astype(F32)
    low_kv = lax.broadcasted_iota(jnp.int32, (2 * w, LANES), 1) < hd
    qi = lax.broadcasted_iota(jnp.int32, (w, w), 0)
    kj = lax.broadcasted_iota(jnp.int32, (w, w), 1)
    from_prev = kj > qi
    low_o = lax.broadcasted_iota(jnp.int32, (w, LANES), 1) < hd
    ones_lo = low_kv.astype(BF16)
    ones_hi = (~low_kv).astype(BF16)
    pairs = SWA_GROUP // 2
    for g in range(SWA_KV_HEADS):
        k_rot = kf if g == 0 else pltpu.roll(kf, hd, axis=1)
        v_rot = vf if g == 0 else pltpu.roll(vf, hd, axis=1)
        k_lo = jnp.where(low_kv, k_rot, 0.0)
        v_lo = jnp.where(low_kv, v_rot, 0.0)
        k_hi = pltpu.roll(k_lo, hd, axis=1).astype(BF16)
        v_hi = pltpu.roll(v_lo, hd, axis=1).astype(BF16)
        vw = (jnp.concatenate([v_lo.astype(BF16), ones_lo], axis=1),
              jnp.concatenate([v_hi, ones_hi], axis=1))
        kk = jnp.concatenate([k_lo.astype(BF16), k_hi], axis=0)
        qst = jnp.concatenate(
            [q_ref[slot, rows, pl.ds((g * pairs + p) * LANES, LANES)] for p in range(pairs)], axis=0)
        s_all = _nt(qst, kk)
        acc = None
        sink_terms = []
        for par in range(2):
            probs, terms = [], []
            for p in range(pairs):
                sink = sink_ref[2 * (g * pairs + p) + par]
                s2 = s_all[p * w:(p + 1) * w, 2 * w * par:2 * w * (par + 1)]
                s_prev = s2[:, :w] if prev_penalty is None else s2[:, :w] + prev_penalty
                s = jnp.where(from_prev, s_prev, s2[:, w:])
                m = jnp.maximum(jnp.max(s, axis=-1, keepdims=True), sink)
                e = jnp.exp(s - m)
                probs.append(jnp.concatenate([jnp.where(from_prev, e, 0.0).astype(BF16),
                                              jnp.where(from_prev, 0.0, e).astype(BF16)], axis=1))
                terms.append(jnp.exp(sink - m))
            part = _nn(jnp.concatenate(probs, axis=0), vw[par])
            acc = part if acc is None else acc + part
            sink_terms.append(terms)
        for p in range(pairs):
            blk = acc[p * w:(p + 1) * w, :]
            denom = blk[:, LANES:] + jnp.where(low_o, sink_terms[0][p], sink_terms[1][p])
            o_ref[rows, pl.ds((g * pairs + p) * LANES, LANES)] = (blk[:, :LANES] / denom).astype(o_ref.dtype)


def _mixer_kernel(sink_ref, x_ref, wqs, wkvs, wqg, wkg, wvg, wgg, wgl, wgk2, bgk, nw,
                  gla_o, swa_o,
                  xb_s, qs_s, kvs_s, qg_s, kg_s, vg_s, gate_s, gk_s, b_s, st_s, kvc_s,
                  *, tiles_per_seq, chunk):
    i = pl.program_id(0)
    ws = lax.rem(i, 2)
    rs = 1 - ws
    seq_start = lax.rem(i + tiles_per_seq - 1, tiles_per_seq) == 0
    tm = x_ref.shape[0]
    proj_scratch = (qs_s, kvs_s, qg_s, kg_s, vg_s, gate_s, gk_s, b_s)

    @pl.when(i == 0)
    def _():
        for s in proj_scratch:
            s[1] = jnp.zeros(s.shape[1:], s.dtype)
        kvc_s[...] = jnp.zeros_like(kvc_s)

    @pl.when(jnp.logical_or(i == 0, seq_start))
    def _():
        st_s[...] = jnp.zeros_like(st_s)

    consts = _gla_consts(chunk)
    row = lax.broadcasted_iota(jnp.int32, (chunk, chunk), 0)
    col = lax.broadcasted_iota(jnp.int32, (chunk, chunk), 1)
    tril = (col <= row).astype(BF16)

    def proj(w_ref, c0=0, width=None):
        width = w_ref.shape[1] if width is None else width
        return _nn(xb_s[...], w_ref[:, c0:c0 + width])

    def plain(w_ref, dst, scale, c0):
        y = proj(w_ref, c0, MXU_COLS)
        dst[ws, :, c0:c0 + MXU_COLS] = (y if scale is None else y * scale).astype(BF16)

    def out_gate(c0):
        g = proj(wgg, c0, MXU_COLS)
        gate_s[ws, :, c0:c0 + MXU_COLS] = (nw[:, c0:c0 + MXU_COLS] * g / (1.0 + jnp.exp(-g))).astype(BF16)

    def forget_gate():
        gl = proj(wgl).astype(BF16)
        z = _nn(gl, wgk2[...]) + bgk[...]
        gk = (jnp.minimum(z, 0.0) - jnp.log1p(jnp.exp(-jnp.abs(z)))) * (LOG2_E / GATE_TAU)
        gk_s[ws] = gk
        gk_hi = gk.astype(BF16)
        gk_lo = (gk - gk_hi.astype(F32)).astype(BF16)
        for r0 in range(0, tm, chunk):
            b_s[ws, r0:r0 + chunk, :] = (_nn(tril, gk_hi[r0:r0 + chunk, :]) + _nn(tril, gk_lo[r0:r0 + chunk, :]))

    def swa(j):
        r0 = j * WINDOW
        if j == 0:
            k_prev, v_prev = kvc_s[:, :SWA_KVW], kvc_s[:, SWA_KVW:]
            penalty = jnp.where(seq_start, NEG_BIG, 0.0)
        else:
            k_prev = kvs_s[rs, r0 - WINDOW:r0, :SWA_KVW]
            v_prev = kvs_s[rs, r0 - WINDOW:r0, SWA_KVW:]
            penalty = None
        _swa_block(sink_ref, qs_s, rs, r0, k_prev, kvs_s[rs, r0:r0 + WINDOW, :SWA_KVW],
                   v_prev, kvs_s[rs, r0:r0 + WINDOW, SWA_KVW:], penalty, swa_o)

    def gla(ci):
        for h in range(GLA_HEADS):
            _gla_chunk_head(qg_s, kg_s, vg_s, b_s, gk_s, gate_s, st_s, gla_o, rs, ci * chunk, h, chunk, consts)

    xb_s[...] = x_ref[...].astype(BF16)
    n_swa = tm // WINDOW
    n_gla = tm // chunk
    mix = []
    for n in range(max(n_swa, n_gla)):
        mix += [functools.partial(swa, n)] if n < n_swa else []
        mix += [functools.partial(gla, n)] if n < n_gla else []
    projs = []
    for w_ref, dst, scale in ((wqs, qs_s, SWA_HEAD_DIM ** -0.5), (wkvs, kvs_s, None), (wqg, qg_s, GLA_DK ** -0.5),
                              (wkg, kg_s, None), (wvg, vg_s, None)):
        projs += [functools.partial(plain, w_ref, dst, scale, c0) for c0 in range(0, w_ref.shape[1], MXU_COLS)]
    projs += [functools.partial(out_gate, c0) for c0 in range(0, D_GLA, MXU_COLS)] + [forget_gate]
    for n, m in enumerate(mix):
        for p in projs[n * len(projs) // len(mix):(n + 1) * len(projs) // len(mix)]:
            p()
        m()
    kvc_s[...] = kvs_s[rs, tm - WINDOW:tm, :]


def _mixer(x2, w_in, w_gk2, b_gk, norm_w, sinks, seq):
    n = x2.shape[0]
    tm = min(MIX_ROWS, seq)
    chunk = min(GLA_CHUNK, tm)
    n_tiles = n // tm
    bounds = [0]
    for s in SPLITS:
        bounds.append(bounds[-1] + s)
    cols = [w_in[:, bounds[i]:bounds[i + 1]].astype(BF16) for i in range(len(SPLITS))]
    wqg, wkg, wvg, wgg, wgl, wqs, wks, wvs = cols
    wkvs = jnp.concatenate([wks, wvs], axis=1)
    wgl = jnp.pad(wgl, ((0, 0), (0, LANES - GATE_RANK)))
    wgk2 = jnp.pad(w_gk2.astype(BF16), ((0, LANES - GATE_RANK), (0, 0)))
    bgk = b_gk.reshape(1, GLA_KW).astype(F32)
    nw = jnp.tile(norm_w.astype(F32), GLA_HEADS).reshape(1, D_GLA)
    weights = (wqs, wkvs, wqg, wkg, wvg, wgg, wgl, wgk2, bgk, nw)
    slot = lambda d, t: pltpu.VMEM((2, tm, d), t)
    out_spec = pl.BlockSpec((tm, D_GLA), lambda i: (jnp.maximum(i - 1, 0), 0))
    return pl.pallas_call(
        functools.partial(_mixer_kernel, tiles_per_seq=seq // tm, chunk=chunk),
        grid=(n_tiles + 1,),
        in_specs=[pl.BlockSpec(memory_space=pltpu.SMEM),
                  pl.BlockSpec((tm, D_MODEL), lambda i: (jnp.minimum(i, n_tiles - 1), 0))]
                 + [_resident(w.shape) for w in weights],
        out_specs=[out_spec, out_spec],
        out_shape=[jax.ShapeDtypeStruct((n, D_GLA), BF16), jax.ShapeDtypeStruct((n, D_SWA), BF16)],
        scratch_shapes=[pltpu.VMEM((tm, D_MODEL), BF16),
                        slot(D_SWA, BF16), slot(2 * SWA_KVW, BF16), slot(GLA_KW, BF16), slot(GLA_KW, BF16),
                        slot(D_GLA, BF16), slot(D_GLA, BF16), slot(GLA_KW, F32), slot(GLA_KW, F32),
                        pltpu.VMEM((GLA_HEADS, GLA_DV, GLA_DK), F32),
                        pltpu.VMEM((WINDOW, 2 * SWA_KVW), BF16)],
        compiler_params=_params("arbitrary"),
        name="token_mixer",
    )(sinks.astype(F32), x2, *weights)


def _out_proj_kernel(gla_ref, swa_ref, x_ref, wa_ref, wb_ref, g_ref, b_ref, o_ref):
    for r0 in range(0, x_ref.shape[0], LN_SUB_ROWS):
        rows = pl.ds(r0, LN_SUB_ROWS)
        mix = _nn(gla_ref[rows, :], wa_ref[...]) + _nn(swa_ref[rows, :], wb_ref[...])
        o_ref[rows, :] = _layer_norm(ALPHA * x_ref[rows, :] + mix, g_ref[...], b_ref[...])


def _out_proj(gla, swa, x2, w_out, ln_g, ln_b):
    n = x2.shape[0]
    tm = min(PROJ_ROWS, n)
    wa = w_out[:D_GLA].astype(BF16)
    wb = w_out[D_GLA:].astype(BF16)
    row = lambda d: pl.BlockSpec((tm, d), lambda i: (i, 0))
    return pl.pallas_call(
        _out_proj_kernel,
        grid=(n // tm,),
        in_specs=[row(D_GLA), row(D_SWA), row(D_MODEL), _resident(wa.shape), _resident(wb.shape),
                  _resident((1, D_MODEL)), _resident((1, D_MODEL))],
        out_specs=row(D_MODEL),
        out_shape=jax.ShapeDtypeStruct((n, D_MODEL), F32),
        compiler_params=_params("parallel"),
        name="out_proj_ln",
    )(gla, swa, x2, wa, wb, ln_g.reshape(1, D_MODEL), ln_b.reshape(1, D_MODEL))


def _mlp_kernel(x_ref, wu_ref, wd_ref, g_ref, b_ref, o_ref, xb_ref):
    f = pl.program_id(1)

    @pl.when(f == 0)
    def _():
        x = x_ref[...]
        xb_ref[...] = x.astype(BF16)
        o_ref[...] = ALPHA * x

    h = jnp.maximum(_nn(xb_ref[...], wu_ref[...]), 0.0)
    o_ref[...] += _nn((h * h).astype(BF16), wd_ref[...])

    @pl.when(f == pl.num_programs(1) - 1)
    def _():
        o_ref[...] = _layer_norm(o_ref[...], g_ref[...], b_ref[...])


def _mlp(x1, w_up, w_down, ln_g, ln_b):
    n = x1.shape[0]
    tm = min(MLP_ROWS, n)
    tf = MLP_FF_TILE
    return pl.pallas_call(
        _mlp_kernel,
        grid=(n // tm, D_FF // tf),
        in_specs=[pl.BlockSpec((tm, D_MODEL), lambda i, f: (i, 0)),
                  pl.BlockSpec((D_MODEL, tf), lambda i, f: (0, f)),
                  pl.BlockSpec((tf, D_MODEL), lambda i, f: (f, 0)),
                  pl.BlockSpec((1, D_MODEL), lambda i, f: (0, 0)),
                  pl.BlockSpec((1, D_MODEL), lambda i, f: (0, 0))],
        out_specs=pl.BlockSpec((tm, D_MODEL), lambda i, f: (i, 0)),
        out_shape=jax.ShapeDtypeStruct((n, D_MODEL), F32),
        scratch_shapes=[pltpu.VMEM((tm, D_MODEL), BF16)],
        compiler_params=_params("parallel", "arbitrary"),
        name="mlp_ln",
    )(x1, w_up.astype(BF16), w_down.astype(BF16), ln_g.reshape(1, D_MODEL), ln_b.reshape(1, D_MODEL))


def kernel(x, w_in, w_gk2, b_gk, gla_norm_w, swa_sinks, w_out, ln1_g, ln1_b, w_up, w_down, ln2_g, ln2_b):
    bsz, seq, _ = x.shape
    n = bsz * seq
    x2 = x.reshape(n, D_MODEL)
    for l in range(DEPTH):
        gla, swa = _mixer(x2, w_in[l], w_gk2[l], b_gk[l], gla_norm_w[l], swa_sinks[l], seq)
        x1 = _out_proj(gla, swa, x2, w_out[l], ln1_g[l], ln1_b[l])
        x2 = _mlp(x1, w_up[l], w_down[l], ln2_g[l], ln2_b[l])
    return x2.reshape(bsz, seq, D_MODEL)
```

```python
import functools

import jax
import jax.numpy as jnp
from jax import lax
from jax.experimental import pallas as pl
from jax.experimental.pallas import tpu as pltpu

F32 = jnp.float32
BF16 = jnp.bfloat16

D_MODEL = 2048
D_GLA = 1024
GLA_HEADS = 4
GLA_DK = 128
GLA_DV = 256
GLA_KW = GLA_HEADS * GLA_DK
GATE_RANK = 16
GATE_TAU = 16.0
SWA_HEAD_DIM = 64
SWA_Q_HEADS = 16
SWA_KV_HEADS = 2
SWA_GROUP = SWA_Q_HEADS // SWA_KV_HEADS
D_SWA = SWA_Q_HEADS * SWA_HEAD_DIM
SWA_KVW = SWA_KV_HEADS * SWA_HEAD_DIM
WINDOW = 128
D_FF = 4 * D_MODEL
SPLITS = (GLA_KW, GLA_KW, D_GLA, D_GLA, GATE_RANK, D_SWA, SWA_KVW, SWA_KVW)
DEPTH = 1
ALPHA = (2 * DEPTH) ** 0.25
LN_EPS = 1e-5
RMS_EPS = 1e-5

LANES = 128
SUBLANES = 8
BF16_ROWS = 16
MXU_COLS = 256
VMEM_LIMIT_BYTES = 56 * 1024 * 1024

MIX_ROWS = 512
LN_SUB_ROWS = 128
GLA_CHUNK = 128
MLP_ROWS = 512
MLP_FF_TILE = 1024
NEG_BIG = -1e30
LOG2_E = 1.4426950408889634

assert MIX_ROWS % GLA_CHUNK == 0 and MIX_ROWS % WINDOW == 0


def _nn(a, b):
    return lax.dot_general(a, b, (((1,), (0,)), ((), ())), preferred_element_type=F32)


def _nt(a, b):
    return lax.dot_general(a, b, (((1,), (1,)), ((), ())), preferred_element_type=F32)


def _tn(a, b):
    return lax.dot_general(a, b, (((0,), (0,)), ((), ())), preferred_element_type=F32)


def _layer_norm(y, g, b):
    mu = jnp.mean(y, axis=-1, keepdims=True)
    d = y - mu
    var = jnp.mean(d * d, axis=-1, keepdims=True)
    return d * lax.rsqrt(var + LN_EPS) * g + b


def _params(*semantics):
    return pltpu.CompilerParams(dimension_semantics=semantics, vmem_limit_bytes=VMEM_LIMIT_BYTES)


def _resident(shape):
    nd = len(shape)
    return pl.BlockSpec(shape, lambda *_: (0,) * nd, pipeline_mode=pl.Buffered(1))


def _gla_chunk_head(q_ref, k_ref, v_ref, b_ref, gk_ref, gate_ref, st_ref, attn_ref, o_ref, slot, base, h, c, consts):
    eye, small, small_mask, row_cd, sub8, col_cc = consts
    dk, dv = GLA_DK, GLA_DV
    rows = pl.ds(base, c)
    kcols = pl.ds(h * dk, dk)
    vcols = pl.ds(h * dv, dv)

    def brow(r):
        return jnp.broadcast_to(b_ref[slot, pl.ds(r, 1), kcols], (SUBLANES, dk))

    def split_expo(s):
        pieces = []
        for r0 in range(0, c, 2 * s):
            mid = jnp.concatenate([brow(base + r0 + s - 1)] * (s // SUBLANES), axis=0)
            pieces += [mid - b[r0:r0 + s, :], b[r0 + s:r0 + 2 * s, :] - mid]
        return jnp.concatenate(pieces, axis=0)

    b = b_ref[slot, rows, kcols]
    q = q_ref[slot, rows, kcols]
    k = k_ref[slot, rows, kcols]
    v = v_ref[slot, rows, vcols]
    b_last = b_ref[slot, pl.ds(base + c - 1, 1), kcols]

    attn = jnp.where(eye, _nt(q, k), 0.0)
    q32 = q.astype(F32)
    k32 = k.astype(F32)
    for s in small:
        upper = (row_cd & s) != 0
        if s == 1:
            expo = jnp.where(upper, gk_ref[slot, rows, kcols], 0.0)
        elif s < SUBLANES:
            tiles = []
            for r0 in range(0, c, SUBLANES):
                mids = [brow(base + r0 + o + s - 1) for o in range(0, SUBLANES, 2 * s)]
                m8 = mids[-1]
                for idx in range(len(mids) - 2, -1, -1):
                    m8 = jnp.where(sub8 < (idx + 1) * 2 * s, mids[idx], m8)
                tiles.append(m8)
            expo = (b - jnp.concatenate(tiles, axis=0)) * jnp.where(upper, 1.0, -1.0)
        else:
            expo = split_expo(s)
        t = (jnp.where(upper, q32, k32) * jnp.exp2(expo)).astype(BF16)
        attn = jnp.where(small_mask[s], _nt(t, t), attn)
    attn_ref[h] = attn

    s = BF16_ROWS
    while 2 * s <= c:
        sel = []
        for r0 in range(0, c, 2 * s):
            sel += [k[r0:r0 + s, :], q[r0 + s:r0 + 2 * s, :]]
        t = jnp.concatenate(sel, axis=0) * jnp.exp2(split_expo(s)).astype(BF16)
        g_lvl = _nt(t, t)
        for r0 in range(0, c, 2 * s):
            lane = lax.broadcasted_iota(jnp.int32, (s, c), 1)
            in_lower = (lane >= r0) & (lane < r0 + s)
            pltpu.store(attn_ref.at[h, pl.ds(r0 + s, s), :], g_lvl[r0 + s:r0 + 2 * s, :], mask=in_lower)
        s *= 2

    st = st_ref[h]
    q_in = q * jnp.exp2(b).astype(BF16)
    o = _nt(q_in, st.astype(BF16)) + _nn(attn_ref[h].astype(BF16), v)
    k_out = k * jnp.exp2(b_last - b).astype(BF16)
    st_ref[h] = st * jnp.exp2(b_last) + _tn(v, k_out)

    o = o * lax.rsqrt(jnp.mean(o * o, axis=-1, keepdims=True) + RMS_EPS)
    o_ref[rows, vcols] = (o * gate_ref[slot, rows, vcols].astype(F32)).astype(o_ref.dtype)


def _gla_consts(c):
    row_cc = lax.broadcasted_iota(jnp.int32, (c, c), 0)
    col_cc = lax.broadcasted_iota(jnp.int32, (c, c), 1)
    eye = row_cc == col_cc
    small = [s for s in (1, 2, 4, 8) if 2 * s <= min(c, BF16_ROWS)]
    small_mask = {s: ((row_cc & s) != 0) & ((col_cc & s) == 0) & ((row_cc ^ col_cc) < 2 * s) for s in small}
    row_cd = lax.broadcasted_iota(jnp.int32, (c, GLA_DK), 0)
    sub8 = lax.broadcasted_iota(jnp.int32, (SUBLANES, GLA_DK), 0)
    return eye, small, small_mask, row_cd, sub8, col_cc


def _swa_block(sink_ref, q_ref, slot, row0, k_prev, k_cur, v_prev, v_cur, prev_penalty, o_ref):
    w = WINDOW
    hd = SWA_HEAD_DIM
    rows = pl.ds(row0, w)
    kf = jnp.concatenate([k_prev, k_cur], axis=0).astype(F32)
    vf = jnp.concatenate([v_prev, v_cur], axis=0).astype(F32)
    low_kv = lax.broadcasted_iota(jnp.int32, (2 * w, LANES), 1) < hd
    qi = lax.broadcasted_iota(jnp.int32, (w, w), 0)
    kj = lax.broadcasted_iota(jnp.int32, (w, w), 1)
    from_prev = kj > qi
    low_o = lax.broadcasted_iota(jnp.int32, (w, LANES), 1) < hd
    ones_lo = low_kv.astype(BF16)
    ones_hi = (~low_kv).astype(BF16)
    pairs = SWA_GROUP // 2
    for g in range(SWA_KV_HEADS):
        k_rot = kf if g == 0 else pltpu.roll(kf, hd, axis=1)
        v_rot = vf if g == 0 else pltpu.roll(vf, hd, axis=1)
        k_lo = jnp.where(low_kv, k_rot, 0.0)
        v_lo = jnp.where(low_kv, v_rot, 0.0)
        k_hi = pltpu.roll(k_lo, hd, axis=1).astype(BF16)
        v_hi = pltpu.roll(v_lo, hd, axis=1).astype(BF16)
        vw = (jnp.concatenate([v_lo.astype(BF16), ones_lo], axis=1),
              jnp.concatenate([v_hi, ones_hi], axis=1))
        kk = jnp.concatenate([k_lo.astype(BF16), k_hi], axis=0)
        qst = jnp.concatenate(
            [q_ref[slot, rows, pl.ds((g * pairs + p) * LANES, LANES)] for p in range(pairs)], axis=0)
        s_all = _nt(qst, kk)
        acc = None
        sink_terms = []
        for par in range(2):
            probs, terms = [], []
            for p in range(pairs):
                sink = sink_ref[2 * (g * pairs + p) + par]
                s2 = s_all[p * w:(p + 1) * w, 2 * w * par:2 * w * (par + 1)]
                s_prev = s2[:, :w] if prev_penalty is None else s2[:, :w] + prev_penalty
                s = jnp.where(from_prev, s_prev, s2[:, w:])
                m = jnp.maximum(jnp.max(s, axis=-1, keepdims=True), sink)
                e = jnp.exp(s - m)
                probs.append(jnp.concatenate([jnp.where(from_prev, e, 0.0).astype(BF16),
                                              jnp.where(from_prev, 0.0, e).astype(BF16)], axis=1))
                terms.append(jnp.exp(sink - m))
            part = _nn(jnp.concatenate(probs, axis=0), vw[par])
            acc = part if acc is None else acc + part
            sink_terms.append(terms)
        for p in range(pairs):
            blk = acc[p * w:(p + 1) * w, :]
            denom = blk[:, LANES:] + jnp.where(low_o, sink_terms[0][p], sink_terms[1][p])
            o_ref[rows, pl.ds((g * pairs + p) * LANES, LANES)] = (blk[:, :LANES] / denom).astype(o_ref.dtype)


def _mixer_kernel(sink_ref, x_ref, wqs, wkvs, wqg, wkg, wvg, wgg, wgl, wgk2, bgk, nw,
                  gla_o, swa_o,
                  xb_s, qs_s, kvs_s, qg_s, kg_s, vg_s, gate_s, gk_s, b_s, st_s, attn_s, kvc_s,
                  *, tiles_per_seq, chunk):
    i = pl.program_id(0)
    ws = lax.rem(i, 2)
    rs = 1 - ws
    seq_start = lax.rem(i + tiles_per_seq - 1, tiles_per_seq) == 0
    tm = x_ref.shape[0]
    proj_scratch = (qs_s, kvs_s, qg_s, kg_s, vg_s, gate_s, gk_s, b_s)

    @pl.when(i == 0)
    def _():
        for s in proj_scratch:
            s[1] = jnp.zeros(s.shape[1:], s.dtype)
        kvc_s[...] = jnp.zeros_like(kvc_s)

    @pl.when(jnp.logical_or(i == 0, seq_start))
    def _():
        st_s[...] = jnp.zeros_like(st_s)

    consts = _gla_consts(chunk)
    row = lax.broadcasted_iota(jnp.int32, (chunk, chunk), 0)
    col = lax.broadcasted_iota(jnp.int32, (chunk, chunk), 1)
    tril = (col <= row).astype(BF16)

    def proj(w_ref, c0=0, width=None):
        width = w_ref.shape[1] if width is None else width
        return _nn(xb_s[...], w_ref[:, c0:c0 + width])

    def plain(w_ref, dst, scale, c0):
        y = proj(w_ref, c0, MXU_COLS)
        dst[ws, :, c0:c0 + MXU_COLS] = (y if scale is None else y * scale).astype(BF16)

    def out_gate(c0):
        g = proj(wgg, c0, MXU_COLS)
        gate_s[ws, :, c0:c0 + MXU_COLS] = (nw[:, c0:c0 + MXU_COLS] * g / (1.0 + jnp.exp(-g))).astype(BF16)

    def forget_gate():
        gl = proj(wgl).astype(BF16)
        z = _nn(gl, wgk2[...]) + bgk[...]
        gk = (jnp.minimum(z, 0.0) - jnp.log1p(jnp.exp(-jnp.abs(z)))) * (LOG2_E / GATE_TAU)
        gk_s[ws] = gk
        gk_hi = gk.astype(BF16)
        gk_lo = (gk - gk_hi.astype(F32)).astype(BF16)
        for r0 in range(0, tm, chunk):
            b_s[ws, r0:r0 + chunk, :] = (_nn(tril, gk_hi[r0:r0 + chunk, :]) + _nn(tril, gk_lo[r0:r0 + chunk, :]))

    def swa(j):
        r0 = j * WINDOW
        if j == 0:
            k_prev, v_prev = kvc_s[:, :SWA_KVW], kvc_s[:, SWA_KVW:]
            penalty = jnp.where(seq_start, NEG_BIG, 0.0)
        else:
            k_prev = kvs_s[rs, r0 - WINDOW:r0, :SWA_KVW]
            v_prev = kvs_s[rs, r0 - WINDOW:r0, SWA_KVW:]
            penalty = None
        _swa_block(sink_ref, qs_s, rs, r0, k_prev, kvs_s[rs, r0:r0 + WINDOW, :SWA_KVW],
                   v_prev, kvs_s[rs, r0:r0 + WINDOW, SWA_KVW:], penalty, swa_o)

    def gla(ci):
        for h in range(GLA_HEADS):
            _gla_chunk_head(qg_s, kg_s, vg_s, b_s, gk_s, gate_s, st_s, attn_s, gla_o, rs, ci * chunk, h, chunk, consts)

    xb_s[...] = x_ref[...].astype(BF16)
    n_swa = tm // WINDOW
    n_gla = tm // chunk
    mix = []
    for n in range(max(n_swa, n_gla)):
        mix += [functools.partial(swa, n)] if n < n_swa else []
        mix += [functools.partial(gla, n)] if n < n_gla else []
    projs = []
    for w_ref, dst, scale in ((wqs, qs_s, SWA_HEAD_DIM ** -0.5), (wkvs, kvs_s, None), (wqg, qg_s, GLA_DK ** -0.5),
                              (wkg, kg_s, None), (wvg, vg_s, None)):
        projs += [functools.partial(plain, w_ref, dst, scale, c0) for c0 in range(0, w_ref.shape[1], MXU_COLS)]
    projs += [functools.partial(out_gate, c0) for c0 in range(0, D_GLA, MXU_COLS)] + [forget_gate]
    for n, m in enumerate(mix):
        for p in projs[n * len(projs) // len(mix):(n + 1) * len(projs) // len(mix)]:
            p()
        m()
    kvc_s[...] = kvs_s[rs, tm - WINDOW:tm, :]


def _mixer(x2, w_in, w_gk2, b_gk, norm_w, sinks, seq):
    n = x2.shape[0]
    tm = min(MIX_ROWS, seq)
    chunk = min(GLA_CHUNK, tm)
    n_tiles = n // tm
    bounds = [0]
    for s in SPLITS:
        bounds.append(bounds[-1] + s)
    cols = [w_in[:, bounds[i]:bounds[i + 1]].astype(BF16) for i in range(len(SPLITS))]
    wqg, wkg, wvg, wgg, wgl, wqs, wks, wvs = cols
    wkvs = jnp.concatenate([wks, wvs], axis=1)
    wgl = jnp.pad(wgl, ((0, 0), (0, LANES - GATE_RANK)))
    wgk2 = jnp.pad(w_gk2.astype(BF16), ((0, LANES - GATE_RANK), (0, 0)))
    bgk = b_gk.reshape(1, GLA_KW).astype(F32)
    nw = jnp.tile(norm_w.astype(F32), GLA_HEADS).reshape(1, D_GLA)
    weights = (wqs, wkvs, wqg, wkg, wvg, wgg, wgl, wgk2, bgk, nw)
    slot = lambda d, t: pltpu.VMEM((2, tm, d), t)
    out_spec = pl.BlockSpec((tm, D_GLA), lambda i: (jnp.maximum(i - 1, 0), 0))
    return pl.pallas_call(
        functools.partial(_mixer_kernel, tiles_per_seq=seq // tm, chunk=chunk),
        grid=(n_tiles + 1,),
        in_specs=[pl.BlockSpec(memory_space=pltpu.SMEM),
                  pl.BlockSpec((tm, D_MODEL), lambda i: (jnp.minimum(i, n_tiles - 1), 0))]
                 + [_resident(w.shape) for w in weights],
        out_specs=[out_spec, out_spec],
        out_shape=[jax.ShapeDtypeStruct((n, D_GLA), BF16), jax.ShapeDtypeStruct((n, D_SWA), BF16)],
        scratch_shapes=[pltpu.VMEM((tm, D_MODEL), BF16),
                        slot(D_SWA, BF16), slot(2 * SWA_KVW, BF16), slot(GLA_KW, BF16), slot(GLA_KW, BF16),
                        slot(D_GLA, BF16), slot(D_GLA, BF16), slot(GLA_KW, F32), slot(GLA_KW, F32),
                        pltpu.VMEM((GLA_HEADS, GLA_DV, GLA_DK), F32),
                        pltpu.VMEM((GLA_HEADS, chunk, chunk), F32),
                        pltpu.VMEM((WINDOW, 2 * SWA_KVW), BF16)],
        compiler_params=_params("arbitrary"),
        name="token_mixer",
    )(sinks.astype(F32), x2, *weights)


def _channel_kernel(gla_ref, swa_ref, x_ref, wa_ref, wb_ref, g1_ref, b1_ref, wu_ref, wd_ref, g2_ref, b2_ref,
                    o_ref, xb_ref):
    f = pl.program_id(1)
    last = pl.num_programs(1) - 1
    tm = x_ref.shape[0]

    def hidden():
        h = jnp.maximum(_nn(xb_ref[...], wu_ref[...]), 0.0)
        return (h * h).astype(BF16)

    @pl.when(f == 0)
    def _():
        for r0 in range(0, tm, LN_SUB_ROWS):
            rows = pl.ds(r0, LN_SUB_ROWS)
            mix = _nn(gla_ref[rows, :], wa_ref[...]) + _nn(swa_ref[rows, :], wb_ref[...])
            x1 = _layer_norm(ALPHA * x_ref[rows, :] + mix, g1_ref[...], b1_ref[...])
            xb_ref[rows, :] = x1.astype(BF16)
            o_ref[rows, :] = ALPHA * x1
        o_ref[...] += _nn(hidden(), wd_ref[...])

    @pl.when(jnp.logical_and(f > 0, f < last))
    def _():
        o_ref[...] += _nn(hidden(), wd_ref[...])

    @pl.when(f == last)
    def _():
        h = hidden()
        for r0 in range(0, tm, LN_SUB_ROWS):
            rows = pl.ds(r0, LN_SUB_ROWS)
            y = o_ref[rows, :] + _nn(h[r0:r0 + LN_SUB_ROWS, :], wd_ref[...])
            o_ref[rows, :] = _layer_norm(y, g2_ref[...], b2_ref[...])


def _channel_mixer(gla, swa, x2, w_out, ln1_g, ln1_b, w_up, w_down, ln2_g, ln2_b):
    n = x2.shape[0]
    tm = min(MLP_ROWS, n)
    tf = MLP_FF_TILE
    assert D_FF // tf >= 2
    wa = w_out[:D_GLA].astype(BF16)
    wb = w_out[D_GLA:].astype(BF16)
    row = lambda d: pl.BlockSpec((tm, d), lambda i, f: (i, 0))
    vec = lambda: _resident((1, D_MODEL))
    as_row = lambda t: t.reshape(1, D_MODEL)
    return pl.pallas_call(
        _channel_kernel,
        grid=(n // tm, D_FF // tf),
        in_specs=[row(D_GLA), row(D_SWA), row(D_MODEL), _resident(wa.shape), _resident(wb.shape), vec(), vec(),
                  pl.BlockSpec((D_MODEL, tf), lambda i, f: (0, f)),
                  pl.BlockSpec((tf, D_MODEL), lambda i, f: (f, 0)), vec(), vec()],
        out_specs=row(D_MODEL),
        out_shape=jax.ShapeDtypeStruct((n, D_MODEL), F32),
        scratch_shapes=[pltpu.VMEM((tm, D_MODEL), BF16)],
        compiler_params=_params("parallel", "arbitrary"),
        name="channel_mixer",
    )(gla, swa, x2, wa, wb, as_row(ln1_g), as_row(ln1_b), w_up.astype(BF16), w_down.astype(BF16),
      as_row(ln2_g), as_row(ln2_b))


def kernel(x, w_in, w_gk2, b_gk, gla_norm_w, swa_sinks, w_out, ln1_g, ln1_b, w_up, w_down, ln2_g, ln2_b):
    bsz, seq, _ = x.shape
    n = bsz * seq
    x2 = x.reshape(n, D_MODEL)
    for l in range(DEPTH):
        gla, swa = _mixer(x2, w_in[l], w_gk2[l], b_gk[l], gla_norm_w[l], swa_sinks[l], seq)
        x2 = _channel_mixer(gla, swa, x2, w_out[l], ln1_g[l], ln1_b[l], w_up[l], w_down[l], ln2_g[l], ln2_b[l])
    return x2.reshape(bsz, seq, D_MODEL)
```

```python
import functools

import jax
import jax.numpy as jnp
from jax import lax
from jax.experimental import pallas as pl
from jax.experimental.pallas import tpu as pltpu

F32 = jnp.float32
BF16 = jnp.bfloat16

D_MODEL = 2048
D_GLA = 1024
GLA_HEADS = 4
GLA_DK = 128
GLA_DV = 256
GLA_KW = GLA_HEADS * GLA_DK
GATE_RANK = 16
GATE_TAU = 16.0
SWA_HEAD_DIM = 64
SWA_Q_HEADS = 16
SWA_KV_HEADS = 2
SWA_GROUP = SWA_Q_HEADS // SWA_KV_HEADS
D_SWA = SWA_Q_HEADS * SWA_HEAD_DIM
SWA_KVW = SWA_KV_HEADS * SWA_HEAD_DIM
WINDOW = 128
D_FF = 4 * D_MODEL
SPLITS = (GLA_KW, GLA_KW, D_GLA, D_GLA, GATE_RANK, D_SWA, SWA_KVW, SWA_KVW)
DEPTH = 1
ALPHA = (2 * DEPTH) ** 0.25
LN_EPS = 1e-5
RMS_EPS = 1e-5

LANES = 128
SUBLANES = 8
MXU_COLS = 256
VMEM_LIMIT_BYTES = 56 * 1024 * 1024

MIX_ROWS = 512
LN_SUB_ROWS = 128
GLA_CHUNK = 128
MLP_ROWS = 512
MLP_FF_TILE = 1024
NEG_BIG = -1e30
LOG2_E = 1.4426950408889634

assert MIX_ROWS % GLA_CHUNK == 0 and MIX_ROWS % WINDOW == 0

W_OFFSET = {"qs": 0, "kvs": D_SWA, "qg": D_SWA + 2 * SWA_KVW}
W_OFFSET["kg"] = W_OFFSET["qg"] + GLA_KW
W_OFFSET["vg"] = W_OFFSET["kg"] + GLA_KW
W_OFFSET["gg"] = W_OFFSET["vg"] + D_GLA
W_OFFSET["gl"] = W_OFFSET["gg"] + D_GLA


def _nn(a, b):
    return lax.dot_general(a, b, (((1,), (0,)), ((), ())), preferred_element_type=F32)


def _nt(a, b):
    return lax.dot_general(a, b, (((1,), (1,)), ((), ())), preferred_element_type=F32)


def _tn(a, b):
    return lax.dot_general(a, b, (((0,), (0,)), ((), ())), preferred_element_type=F32)


def _layer_norm(y, g, b):
    mu = jnp.mean(y, axis=-1, keepdims=True)
    d = y - mu
    var = jnp.mean(d * d, axis=-1, keepdims=True)
    return d * lax.rsqrt(var + LN_EPS) * g + b


def _params(*semantics):
    return pltpu.CompilerParams(dimension_semantics=semantics, vmem_limit_bytes=VMEM_LIMIT_BYTES)


def _resident(shape):
    nd = len(shape)
    return pl.BlockSpec(shape, lambda *_: (0,) * nd, pipeline_mode=pl.Buffered(1))


def _gla_chunk_head(q_ref, k_ref, v_ref, b_ref, gk_ref, gate_ref, st_ref, o_ref, slot, base, h, c, consts):
    eye, small, small_mask, row_cd, sub8, lane8 = consts
    dk, dv = GLA_DK, GLA_DV
    rows = pl.ds(base, c)
    kcols = pl.ds(h * dk, dk)
    vcols = pl.ds(h * dv, dv)

    def brow(r):
        return jnp.broadcast_to(b_ref[slot, pl.ds(r, 1), kcols], (SUBLANES, dk))

    b = b_ref[slot, rows, kcols]
    qb = q_ref[slot, rows, kcols]
    kb = k_ref[slot, rows, kcols]
    q = qb.astype(F32)
    k = kb.astype(F32)
    v = v_ref[slot, rows, vcols]
    b_last = b_ref[slot, pl.ds(base + c - 1, 1), kcols]

    attn = jnp.where(eye, _nt(qb, kb), 0.0)
    for s in small:
        upper = (row_cd & s) != 0
        if s == 1:
            expo = jnp.where(upper, gk_ref[slot, rows, kcols], 0.0)
        else:
            tiles = []
            for r0 in range(0, c, SUBLANES):
                mids = [brow(base + r0 + o + s - 1) for o in range(0, SUBLANES, 2 * s)]
                m8 = mids[-1]
                for idx in range(len(mids) - 2, -1, -1):
                    m8 = jnp.where(sub8 < (idx + 1) * 2 * s, mids[idx], m8)
                tiles.append(m8)
            expo = (b - jnp.concatenate(tiles, axis=0)) * jnp.where(upper, 1.0, -1.0)
        t = (jnp.where(upper, q, k) * jnp.exp2(expo)).astype(BF16)
        attn = jnp.where(small_mask[s], _nt(t, t), attn)

    slabs = [attn[r:r + SUBLANES, :] for r in range(0, c, SUBLANES)]
    s = SUBLANES
    while 2 * s <= c:
        sel, expo = [], []
        for r0 in range(0, c, 2 * s):
            mid = jnp.concatenate([brow(base + r0 + s - 1)] * (s // SUBLANES), axis=0)
            sel += [k[r0:r0 + s, :], q[r0 + s:r0 + 2 * s, :]]
            expo += [mid - b[r0:r0 + s, :], b[r0 + s:r0 + 2 * s, :] - mid]
        t = (jnp.concatenate(sel, axis=0) * jnp.exp2(jnp.concatenate(expo, axis=0))).astype(BF16)
        g_lvl = _nt(t, t)
        for r0 in range(0, c, 2 * s):
            in_lower = (lane8 >= r0) & (lane8 < r0 + s)
            for r in range(r0 + s, r0 + 2 * s, SUBLANES):
                slabs[r // SUBLANES] = jnp.where(in_lower, g_lvl[r:r + SUBLANES, :], slabs[r // SUBLANES])
        s *= 2
    attn = jnp.concatenate(slabs, axis=0).astype(BF16)

    st = st_ref[h]
    q_in = (q * jnp.exp2(b)).astype(BF16)
    o = _nt(q_in, st.astype(BF16)) + _nn(attn, v)
    k_out = (k * jnp.exp2(b_last - b)).astype(BF16)
    st_ref[h] = st * jnp.exp2(b_last) + _tn(v, k_out)

    o = o * lax.rsqrt(jnp.mean(o * o, axis=-1, keepdims=True) + RMS_EPS)
    o_ref[rows, vcols] = (o * gate_ref[slot, rows, vcols].astype(F32)).astype(o_ref.dtype)


def _gla_consts(c):
    row_cc = lax.broadcasted_iota(jnp.int32, (c, c), 0)
    col_cc = lax.broadcasted_iota(jnp.int32, (c, c), 1)
    eye = row_cc == col_cc
    small = [s for s in (1, 2, 4) if 2 * s <= min(c, SUBLANES)]
    small_mask = {s: ((row_cc & s) != 0) & ((col_cc & s) == 0) & ((row_cc ^ col_cc) < 2 * s) for s in small}
    row_cd = lax.broadcasted_iota(jnp.int32, (c, GLA_DK), 0)
    sub8 = lax.broadcasted_iota(jnp.int32, (SUBLANES, GLA_DK), 0)
    lane8 = lax.broadcasted_iota(jnp.int32, (SUBLANES, c), 1)
    return eye, small, small_mask, row_cd, sub8, lane8


def _swa_block(sink_ref, q_ref, slot, row0, k_prev, k_cur, v_prev, v_cur, prev_penalty, o_ref):
    w = WINDOW
    hd = SWA_HEAD_DIM
    rows = pl.ds(row0, w)
    kf = jnp.concatenate([k_prev, k_cur], axis=0).astype(F32)
    vf = jnp.concatenate([v_prev, v_cur], axis=0).astype(F32)
    low_kv = lax.broadcasted_iota(jnp.int32, (2 * w, LANES), 1) < hd
    qi = lax.broadcasted_iota(jnp.int32, (w, w), 0)
    kj = lax.broadcasted_iota(jnp.int32, (w, w), 1)
    from_prev = kj > qi
    low_o = lax.broadcasted_iota(jnp.int32, (w, LANES), 1) < hd
    ones_lo = low_kv.astype(BF16)
    ones_hi = (~low_kv).astype(BF16)
    pairs = SWA_GROUP // 2
    for g in range(SWA_KV_HEADS):
        k_rot = kf if g == 0 else pltpu.roll(kf, hd, axis=1)
        v_rot = vf if g == 0 else pltpu.roll(vf, hd, axis=1)
        k_lo = jnp.where(low_kv, k_rot, 0.0)
        v_lo = jnp.where(low_kv, v_rot, 0.0)
        k_hi = pltpu.roll(k_lo, hd, axis=1).astype(BF16)
        v_hi = pltpu.roll(v_lo, hd, axis=1).astype(BF16)
        vw = (jnp.concatenate([v_lo.astype(BF16), ones_lo], axis=1),
              jnp.concatenate([v_hi, ones_hi], axis=1))
        kk = jnp.concatenate([k_lo.astype(BF16), k_hi], axis=0)
        qst = jnp.concatenate(
            [q_ref[slot, rows, pl.ds((g * pairs + p) * LANES, LANES)] for p in range(pairs)], axis=0)
        s_all = _nt(qst, kk)
        acc = None
        sink_terms = []
        for par in range(2):
            probs, terms = [], []
            for p in range(pairs):
                sink = sink_ref[2 * (g * pairs + p) + par] * LOG2_E
                s2 = s_all[p * w:(p + 1) * w, 2 * w * par:2 * w * (par + 1)]
                s_prev = s2[:, :w] if prev_penalty is None else s2[:, :w] + prev_penalty
                s = jnp.where(from_prev, s_prev, s2[:, w:])
                m = jnp.maximum(jnp.max(s, axis=-1, keepdims=True), sink)
                e = jnp.exp2(s - m)
                probs.append(jnp.concatenate([jnp.where(from_prev, e, 0.0).astype(BF16),
                                              jnp.where(from_prev, 0.0, e).astype(BF16)], axis=1))
                terms.append(jnp.exp2(sink - m))
            part = _nn(jnp.concatenate(probs, axis=0), vw[par])
            acc = part if acc is None else acc + part
            sink_terms.append(terms)
        for p in range(pairs):
            blk = acc[p * w:(p + 1) * w, :]
            denom = blk[:, LANES:] + jnp.where(low_o, sink_terms[0][p], sink_terms[1][p])
            o_ref[rows, pl.ds((g * pairs + p) * LANES, LANES)] = (blk[:, :LANES] / denom).astype(o_ref.dtype)


def _mixer_kernel(sink_ref, x_ref, w_ref, wgk2, bgk, nw,
                  gla_o, swa_o,
                  xb_s, qs_s, kvs_s, qg_s, kg_s, vg_s, gate_s, gk_s, b_s, st_s, kvc_s,
                  *, tiles_per_seq, chunk):
    i = pl.program_id(0)
    ws = lax.rem(i, 2)
    rs = 1 - ws
    seq_start = lax.rem(i + tiles_per_seq - 1, tiles_per_seq) == 0
    tm = x_ref.shape[0]
    proj_scratch = (qs_s, kvs_s, qg_s, kg_s, vg_s, gate_s, gk_s, b_s)

    @pl.when(i == 0)
    def _():
        for s in proj_scratch:
            s[1] = jnp.zeros(s.shape[1:], s.dtype)
        kvc_s[...] = jnp.zeros_like(kvc_s)

    @pl.when(jnp.logical_or(i == 0, seq_start))
    def _():
        st_s[...] = jnp.zeros_like(st_s)

    consts = _gla_consts(chunk)
    row = lax.broadcasted_iota(jnp.int32, (chunk, chunk), 0)
    col = lax.broadcasted_iota(jnp.int32, (chunk, chunk), 1)
    tril = (col <= row).astype(BF16)

    def proj(group, c0, width):
        w0 = W_OFFSET[group] + c0
        return _nn(xb_s[...], w_ref[:, w0:w0 + width])

    def plain(group, dst, scale, c0):
        y = proj(group, c0, MXU_COLS)
        dst[ws, :, c0:c0 + MXU_COLS] = (y if scale is None else y * scale).astype(BF16)

    def out_gate(c0):
        g = proj("gg", c0, MXU_COLS)
        gate_s[ws, :, c0:c0 + MXU_COLS] = (nw[:, c0:c0 + MXU_COLS] * g / (1.0 + jnp.exp(-g))).astype(BF16)

    def forget_gate():
        gl = proj("gl", 0, LANES).astype(BF16)
        z = _nn(gl, wgk2[...]) + bgk[...]
        gk = (jnp.minimum(z, 0.0) - jnp.log1p(jnp.exp(-jnp.abs(z)))) * (LOG2_E / GATE_TAU)
        gk_s[ws] = gk
        gk_hi = gk.astype(BF16)
        gk_lo = (gk - gk_hi.astype(F32)).astype(BF16)
        for r0 in range(0, tm, chunk):
            b_s[ws, r0:r0 + chunk, :] = (_nn(tril, gk_hi[r0:r0 + chunk, :]) + _nn(tril, gk_lo[r0:r0 + chunk, :]))

    def swa(j):
        r0 = j * WINDOW
        if j == 0:
            k_prev, v_prev = kvc_s[:, :SWA_KVW], kvc_s[:, SWA_KVW:]
            penalty = jnp.where(seq_start, NEG_BIG, 0.0)
        else:
            k_prev = kvs_s[rs, r0 - WINDOW:r0, :SWA_KVW]
            v_prev = kvs_s[rs, r0 - WINDOW:r0, SWA_KVW:]
            penalty = None
        _swa_block(sink_ref, qs_s, rs, r0, k_prev, kvs_s[rs, r0:r0 + WINDOW, :SWA_KVW],
                   v_prev, kvs_s[rs, r0:r0 + WINDOW, SWA_KVW:], penalty, swa_o)

    def gla(ci):
        for h in range(GLA_HEADS):
            _gla_chunk_head(qg_s, kg_s, vg_s, b_s, gk_s, gate_s, st_s, gla_o, rs, ci * chunk, h, chunk, consts)

    xb_s[...] = x_ref[...].astype(BF16)
    n_swa = tm // WINDOW
    n_gla = tm // chunk
    mix = []
    for n in range(max(n_swa, n_gla)):
        mix += [functools.partial(swa, n)] if n < n_swa else []
        mix += [functools.partial(gla, n)] if n < n_gla else []
    projs = []
    for group, dst, scale in (("qs", qs_s, SWA_HEAD_DIM ** -0.5 * LOG2_E), ("kvs", kvs_s, None),
                              ("qg", qg_s, GLA_DK ** -0.5), ("kg", kg_s, None), ("vg", vg_s, None)):
        projs += [functools.partial(plain, group, dst, scale, c0) for c0 in range(0, dst.shape[2], MXU_COLS)]
    projs += [functools.partial(out_gate, c0) for c0 in range(0, D_GLA, MXU_COLS)] + [forget_gate]
    for n, m in enumerate(mix):
        for p in projs[n * len(projs) // len(mix):(n + 1) * len(projs) // len(mix)]:
            p()
        m()
    kvc_s[...] = kvs_s[rs, tm - WINDOW:tm, :]


def _mixer(x2, w_in, w_gk2, b_gk, norm_w, sinks, seq):
    n = x2.shape[0]
    tm = min(MIX_ROWS, seq)
    chunk = min(GLA_CHUNK, tm)
    n_tiles = n // tm
    swa0 = sum(SPLITS[:5])
    w_all = jnp.concatenate([w_in[:, swa0:], w_in[:, :swa0], jnp.zeros((D_MODEL, LANES - GATE_RANK), w_in.dtype)],
                            axis=1).astype(BF16)
    wgk2 = jnp.pad(w_gk2.astype(BF16), ((0, LANES - GATE_RANK), (0, 0)))
    bgk = b_gk.reshape(1, GLA_KW).astype(F32)
    nw = jnp.tile(norm_w.astype(F32), GLA_HEADS).reshape(1, D_GLA)
    weights = (w_all, wgk2, bgk, nw)
    slot = lambda d, t: pltpu.VMEM((2, tm, d), t)
    out_spec = pl.BlockSpec((tm, D_GLA), lambda i: (jnp.maximum(i - 1, 0), 0))
    return pl.pallas_call(
        functools.partial(_mixer_kernel, tiles_per_seq=seq // tm, chunk=chunk),
        grid=(n_tiles + 1,),
        in_specs=[pl.BlockSpec(memory_space=pltpu.SMEM),
                  pl.BlockSpec((tm, D_MODEL), lambda i: (jnp.minimum(i, n_tiles - 1), 0))]
                 + [_resident(w.shape) for w in weights],
        out_specs=[out_spec, out_spec],
        out_shape=[jax.ShapeDtypeStruct((n, D_GLA), BF16), jax.ShapeDtypeStruct((n, D_SWA), BF16)],
        scratch_shapes=[pltpu.VMEM((tm, D_MODEL), BF16),
                        slot(D_SWA, BF16), slot(2 * SWA_KVW, BF16), slot(GLA_KW, BF16), slot(GLA_KW, BF16),
                        slot(D_GLA, BF16), slot(D_GLA, BF16), slot(GLA_KW, F32), slot(GLA_KW, F32),
                        pltpu.VMEM((GLA_HEADS, GLA_DV, GLA_DK), F32),
                        pltpu.VMEM((WINDOW, 2 * SWA_KVW), BF16)],
        compiler_params=_params("arbitrary"),
        name="token_mixer",
    )(sinks.astype(F32), x2, *weights)


def _channel_kernel(gla_ref, swa_ref, x_ref, wo_ref, g1_ref, b1_ref, wu_ref, wd_ref, g2_ref, b2_ref,
                    o_ref, xb_ref):
    f = pl.program_id(1)
    last = pl.num_programs(1) - 1
    tm = x_ref.shape[0]

    def hidden():
        h = jnp.maximum(_nn(xb_ref[...], wu_ref[...]), 0.0)
        return (h * h).astype(BF16)

    @pl.when(f == 0)
    def _():
        for r0 in range(0, tm, LN_SUB_ROWS):
            rows = pl.ds(r0, LN_SUB_ROWS)
            mix = _nn(gla_ref[rows, :], wo_ref[:D_GLA, :]) + _nn(swa_ref[rows, :], wo_ref[D_GLA:, :])
            x1 = _layer_norm(ALPHA * x_ref[rows, :] + mix, g1_ref[...], b1_ref[...])
            xb_ref[rows, :] = x1.astype(BF16)
            o_ref[rows, :] = ALPHA * x1
        o_ref[...] += _nn(hidden(), wd_ref[...])

    @pl.when(jnp.logical_and(f > 0, f < last))
    def _():
        o_ref[...] += _nn(hidden(), wd_ref[...])

    @pl.when(f == last)
    def _():
        h = hidden()
        for r0 in range(0, tm, LN_SUB_ROWS):
            rows = pl.ds(r0, LN_SUB_ROWS)
            y = o_ref[rows, :] + _nn(h[r0:r0 + LN_SUB_ROWS, :], wd_ref[...])
            o_ref[rows, :] = _layer_norm(y, g2_ref[...], b2_ref[...])


def _channel_mixer(gla, swa, x2, w_out, ln1_g, ln1_b, w_up, w_down, ln2_g, ln2_b):
    n = x2.shape[0]
    tm = min(MLP_ROWS, n)
    tf = MLP_FF_TILE
    assert D_FF // tf >= 2
    row = lambda d: pl.BlockSpec((tm, d), lambda i, f: (i, 0))
    vec = lambda: _resident((1, D_MODEL))
    as_row = lambda t: t.reshape(1, D_MODEL)
    return pl.pallas_call(
        _channel_kernel,
        grid=(n // tm, D_FF // tf),
        in_specs=[row(D_GLA), row(D_SWA), row(D_MODEL), _resident(w_out.shape), vec(), vec(),
                  pl.BlockSpec((D_MODEL, tf), lambda i, f: (0, f)),
                  pl.BlockSpec((tf, D_MODEL), lambda i, f: (f, 0)), vec(), vec()],
        out_specs=row(D_MODEL),
        out_shape=jax.ShapeDtypeStruct((n, D_MODEL), F32),
        scratch_shapes=[pltpu.VMEM((tm, D_MODEL), BF16)],
        compiler_params=_params("parallel", "arbitrary"),
        name="channel_mixer",
    )(gla, swa, x2, w_out.astype(BF16), as_row(ln1_g), as_row(ln1_b), w_up.astype(BF16), w_down.astype(BF16),
      as_row(ln2_g), as_row(ln2_b))


def kernel(x, w_in, w_gk2, b_gk, gla_norm_w, swa_sinks, w_out, ln1_g, ln1_b, w_up, w_down, ln2_g, ln2_b):
    bsz, seq, _ = x.shape
    n = bsz * seq
    x2 = x.reshape(n, D_MODEL)
    for l in range(DEPTH):
        gla, swa = _mixer(x2, w_in[l], w_gk2[l], b_gk[l], gla_norm_w[l], swa_sinks[l], seq)
        x2 = _channel_mixer(gla, swa, x2, w_out[l], ln1_g[l], ln1_b[l], w_up[l], w_down[l], ln2_g[l], ln2_b[l])
    return x2.reshape(bsz, seq, D_MODEL)
```

```python
import functools

import jax
import jax.numpy as jnp
from jax import lax
from jax.experimental import pallas as pl
from jax.experimental.pallas import tpu as pltpu

F32 = jnp.float32
BF16 = jnp.bfloat16

D_MODEL = 2048
D_GLA = 1024
GLA_HEADS = 4
GLA_DK = 128
GLA_DV = 256
GLA_KW = GLA_HEADS * GLA_DK
GATE_RANK = 16
GATE_TAU = 16.0
SWA_HEAD_DIM = 64
SWA_Q_HEADS = 16
SWA_KV_HEADS = 2
SWA_GROUP = SWA_Q_HEADS // SWA_KV_HEADS
D_SWA = SWA_Q_HEADS * SWA_HEAD_DIM
SWA_KVW = SWA_KV_HEADS * SWA_HEAD_DIM
WINDOW = 128
D_FF = 4 * D_MODEL
SPLITS = (GLA_KW, GLA_KW, D_GLA, D_GLA, GATE_RANK, D_SWA, SWA_KVW, SWA_KVW)
DEPTH = 1
ALPHA = (2 * DEPTH) ** 0.25
LN_EPS = 1e-5
RMS_EPS = 1e-5

LANES = 128
SUBLANES = 8
MXU_COLS = 256
VMEM_LIMIT_BYTES = 56 * 1024 * 1024

MIX_ROWS = 256
LN_SUB_ROWS = 128
GLA_CHUNK = 128
MLP_ROWS = 512
MLP_FF_TILE = 1024
NEG_BIG = -1e30
LOG2_E = 1.4426950408889634

assert MIX_ROWS % GLA_CHUNK == 0 and MIX_ROWS % WINDOW == 0

def _nn(a, b):
    return lax.dot_general(a, b, (((1,), (0,)), ((), ())), preferred_element_type=F32)


def _nt(a, b):
    return lax.dot_general(a, b, (((1,), (1,)), ((), ())), preferred_element_type=F32)


def _tn(a, b):
    return lax.dot_general(a, b, (((0,), (0,)), ((), ())), preferred_element_type=F32)


def _layer_norm(y, g, b):
    mu = jnp.mean(y, axis=-1, keepdims=True)
    d = y - mu
    var = jnp.mean(d * d, axis=-1, keepdims=True)
    return d * lax.rsqrt(var + LN_EPS) * g + b


def _params(*semantics):
    return pltpu.CompilerParams(dimension_semantics=semantics, vmem_limit_bytes=VMEM_LIMIT_BYTES)


def _resident(shape):
    nd = len(shape)
    return pl.BlockSpec(shape, lambda *_: (0,) * nd, pipeline_mode=pl.Buffered(1))


def _gla_chunk_head(q_ref, k_ref, v_ref, b_ref, gk_ref, gate_ref, st_ref, o_ref, slot, base, h, c, consts):
    eye, small, small_mask, row_cd, sub8, lane8 = consts
    dk, dv = GLA_DK, GLA_DV
    rows = pl.ds(base, c)
    kcols = pl.ds(h * dk, dk)
    vcols = pl.ds(h * dv, dv)

    def brow(r):
        return jnp.broadcast_to(b_ref[slot, pl.ds(r, 1), kcols], (SUBLANES, dk))

    b = b_ref[slot, rows, kcols]
    qb = q_ref[slot, rows, kcols]
    kb = k_ref[slot, rows, kcols]
    q = qb.astype(F32)
    k = kb.astype(F32)
    v = v_ref[slot, rows, vcols]
    b_last = b_ref[slot, pl.ds(base + c - 1, 1), kcols]

    attn = jnp.where(eye, _nt(qb, kb), 0.0)
    for s in small:
        upper = (row_cd & s) != 0
        if s == 1:
            expo = jnp.where(upper, gk_ref[slot, rows, kcols], 0.0)
        else:
            tiles = []
            for r0 in range(0, c, SUBLANES):
                mids = [brow(base + r0 + o + s - 1) for o in range(0, SUBLANES, 2 * s)]
                m8 = mids[-1]
                for idx in range(len(mids) - 2, -1, -1):
                    m8 = jnp.where(sub8 < (idx + 1) * 2 * s, mids[idx], m8)
                tiles.append(m8)
            expo = (b - jnp.concatenate(tiles, axis=0)) * jnp.where(upper, 1.0, -1.0)
        t = (jnp.where(upper, q, k) * jnp.exp2(expo)).astype(BF16)
        attn = jnp.where(small_mask[s], _nt(t, t), attn)

    slabs = [attn[r:r + SUBLANES, :] for r in range(0, c, SUBLANES)]
    s = SUBLANES
    while 2 * s <= c:
        sel, expo = [], []
        for r0 in range(0, c, 2 * s):
            mid = jnp.concatenate([brow(base + r0 + s - 1)] * (s // SUBLANES), axis=0)
            sel += [k[r0:r0 + s, :], q[r0 + s:r0 + 2 * s, :]]
            expo += [mid - b[r0:r0 + s, :], b[r0 + s:r0 + 2 * s, :] - mid]
        t = (jnp.concatenate(sel, axis=0) * jnp.exp2(jnp.concatenate(expo, axis=0))).astype(BF16)
        g_lvl = _nt(t, t)
        for r0 in range(0, c, 2 * s):
            in_lower = (lane8 >= r0) & (lane8 < r0 + s)
            for r in range(r0 + s, r0 + 2 * s, SUBLANES):
                slabs[r // SUBLANES] = jnp.where(in_lower, g_lvl[r:r + SUBLANES, :], slabs[r // SUBLANES])
        s *= 2
    attn = jnp.concatenate(slabs, axis=0).astype(BF16)

    st = st_ref[h]
    q_in = (q * jnp.exp2(b)).astype(BF16)
    o = _nt(q_in, st.astype(BF16)) + _nn(attn, v)
    k_out = (k * jnp.exp2(b_last - b)).astype(BF16)
    st_ref[h] = st * jnp.exp2(b_last) + _tn(v, k_out)

    o = o * lax.rsqrt(jnp.mean(o * o, axis=-1, keepdims=True) + RMS_EPS)
    o_ref[rows, vcols] = (o * gate_ref[slot, rows, vcols].astype(F32)).astype(o_ref.dtype)


def _gla_consts(c):
    row_cc = lax.broadcasted_iota(jnp.int32, (c, c), 0)
    col_cc = lax.broadcasted_iota(jnp.int32, (c, c), 1)
    eye = row_cc == col_cc
    small = [s for s in (1, 2, 4) if 2 * s <= min(c, SUBLANES)]
    small_mask = {s: ((row_cc & s) != 0) & ((col_cc & s) == 0) & ((row_cc ^ col_cc) < 2 * s) for s in small}
    row_cd = lax.broadcasted_iota(jnp.int32, (c, GLA_DK), 0)
    sub8 = lax.broadcasted_iota(jnp.int32, (SUBLANES, GLA_DK), 0)
    lane8 = lax.broadcasted_iota(jnp.int32, (SUBLANES, c), 1)
    return eye, small, small_mask, row_cd, sub8, lane8


def _swa_block(sink_ref, q_ref, slot, row0, k_prev, k_cur, v_prev, v_cur, prev_penalty, o_ref):
    w = WINDOW
    hd = SWA_HEAD_DIM
    rows = pl.ds(row0, w)
    kf = jnp.concatenate([k_prev, k_cur], axis=0).astype(F32)
    vf = jnp.concatenate([v_prev, v_cur], axis=0).astype(F32)
    low_kv = lax.broadcasted_iota(jnp.int32, (2 * w, LANES), 1) < hd
    qi = lax.broadcasted_iota(jnp.int32, (w, w), 0)
    kj = lax.broadcasted_iota(jnp.int32, (w, w), 1)
    from_prev = kj > qi
    low_o = lax.broadcasted_iota(jnp.int32, (w, LANES), 1) < hd
    ones_lo = low_kv.astype(BF16)
    ones_hi = (~low_kv).astype(BF16)
    pairs = SWA_GROUP // 2
    for g in range(SWA_KV_HEADS):
        k_rot = kf if g == 0 else pltpu.roll(kf, hd, axis=1)
        v_rot = vf if g == 0 else pltpu.roll(vf, hd, axis=1)
        k_lo = jnp.where(low_kv, k_rot, 0.0)
        v_lo = jnp.where(low_kv, v_rot, 0.0)
        k_hi = pltpu.roll(k_lo, hd, axis=1).astype(BF16)
        v_hi = pltpu.roll(v_lo, hd, axis=1).astype(BF16)
        vw = (jnp.concatenate([v_lo.astype(BF16), ones_lo], axis=1),
              jnp.concatenate([v_hi, ones_hi], axis=1))
        kk = jnp.concatenate([k_lo.astype(BF16), k_hi], axis=0)
        qst = jnp.concatenate(
            [q_ref[slot, rows, pl.ds((g * pairs + p) * LANES, LANES)] for p in range(pairs)], axis=0)
        s_all = _nt(qst, kk)
        acc = None
        sink_terms = []
        for par in range(2):
            probs, terms = [], []
            for p in range(pairs):
                sink = sink_ref[2 * (g * pairs + p) + par] * LOG2_E
                s2 = s_all[p * w:(p + 1) * w, 2 * w * par:2 * w * (par + 1)]
                s_prev = s2[:, :w] if prev_penalty is None else s2[:, :w] + prev_penalty
                s = jnp.where(from_prev, s_prev, s2[:, w:])
                m = jnp.maximum(jnp.max(s, axis=-1, keepdims=True), sink)
                e = jnp.exp2(s - m)
                probs.append(jnp.concatenate([jnp.where(from_prev, e, 0.0).astype(BF16),
                                              jnp.where(from_prev, 0.0, e).astype(BF16)], axis=1))
                terms.append(jnp.exp2(sink - m))
            part = _nn(jnp.concatenate(probs, axis=0), vw[par])
            acc = part if acc is None else acc + part
            sink_terms.append(terms)
        for p in range(pairs):
            blk = acc[p * w:(p + 1) * w, :]
            denom = blk[:, LANES:] + jnp.where(low_o, sink_terms[0][p], sink_terms[1][p])
            o_ref[rows, pl.ds((g * pairs + p) * LANES, LANES)] = (blk[:, :LANES] / denom).astype(o_ref.dtype)


def _mixer_kernel(sink_ref, x_ref, wqs, wkvs, wqg, wkg, wvg, wgg, wgl, wgk2, bgk, nw, wo_f, wu_f, wd_f,
                  gla_o, swa_o, wo_b, wu_b, wd_b,
                  xb_s, qs_s, kvs_s, qg_s, kg_s, vg_s, gate_s, gk_s, b_s, st_s, kvc_s,
                  *, tiles_per_seq, chunk):
    for src, dst in ((wo_f, wo_b), (wu_f, wu_b), (wd_f, wd_b)):
        dst[...] = src[...].astype(BF16)

    i = pl.program_id(0)
    ws = lax.rem(i, 2)
    rs = 1 - ws
    seq_start = lax.rem(i + tiles_per_seq - 1, tiles_per_seq) == 0
    tm = x_ref.shape[0]
    proj_scratch = (qs_s, kvs_s, qg_s, kg_s, vg_s, gate_s, gk_s, b_s)

    @pl.when(i == 0)
    def _():
        for s in proj_scratch:
            s[1] = jnp.zeros(s.shape[1:], s.dtype)
        kvc_s[...] = jnp.zeros_like(kvc_s)

    @pl.when(jnp.logical_or(i == 0, seq_start))
    def _():
        st_s[...] = jnp.zeros_like(st_s)

    consts = _gla_consts(chunk)
    row = lax.broadcasted_iota(jnp.int32, (chunk, chunk), 0)
    col = lax.broadcasted_iota(jnp.int32, (chunk, chunk), 1)
    tril = (col <= row).astype(BF16)

    w_refs = {"qs": wqs, "kvs": wkvs, "qg": wqg, "kg": wkg, "vg": wvg, "gg": wgg, "gl": wgl}

    def proj(group, c0, width):
        return _nn(xb_s[...], w_refs[group][:, c0:c0 + width])

    def plain(group, dst, scale, c0):
        y = proj(group, c0, MXU_COLS)
        dst[ws, :, c0:c0 + MXU_COLS] = (y if scale is None else y * scale).astype(BF16)

    def out_gate(c0):
        g = proj("gg", c0, MXU_COLS)
        gate_s[ws, :, c0:c0 + MXU_COLS] = (nw[:, c0:c0 + MXU_COLS] * g / (1.0 + jnp.exp(-g))).astype(BF16)

    def forget_gate():
        gl = proj("gl", 0, LANES).astype(BF16)
        z = _nn(gl, wgk2[...]) + bgk[...]
        gk = (jnp.minimum(z, 0.0) - jnp.log1p(jnp.exp(-jnp.abs(z)))) * (LOG2_E / GATE_TAU)
        gk_s[ws] = gk
        gk_hi = gk.astype(BF16)
        gk_lo = (gk - gk_hi.astype(F32)).astype(BF16)
        for r0 in range(0, tm, chunk):
            b_s[ws, r0:r0 + chunk, :] = (_nn(tril, gk_hi[r0:r0 + chunk, :]) + _nn(tril, gk_lo[r0:r0 + chunk, :]))

    def swa(j):
        r0 = j * WINDOW
        if j == 0:
            k_prev, v_prev = kvc_s[:, :SWA_KVW], kvc_s[:, SWA_KVW:]
            penalty = jnp.where(seq_start, NEG_BIG, 0.0)
        else:
            k_prev = kvs_s[rs, r0 - WINDOW:r0, :SWA_KVW]
            v_prev = kvs_s[rs, r0 - WINDOW:r0, SWA_KVW:]
            penalty = None
        _swa_block(sink_ref, qs_s, rs, r0, k_prev, kvs_s[rs, r0:r0 + WINDOW, :SWA_KVW],
                   v_prev, kvs_s[rs, r0:r0 + WINDOW, SWA_KVW:], penalty, swa_o)

    def gla(ci):
        for h in range(GLA_HEADS):
            _gla_chunk_head(qg_s, kg_s, vg_s, b_s, gk_s, gate_s, st_s, gla_o, rs, ci * chunk, h, chunk, consts)

    xb_s[...] = x_ref[...].astype(BF16)
    n_swa = tm // WINDOW
    n_gla = tm // chunk
    mix = []
    for n in range(max(n_swa, n_gla)):
        mix += [functools.partial(swa, n)] if n < n_swa else []
        mix += [functools.partial(gla, n)] if n < n_gla else []
    projs = []
    for group, dst, scale in (("qs", qs_s, SWA_HEAD_DIM ** -0.5 * LOG2_E), ("kvs", kvs_s, None),
                              ("qg", qg_s, GLA_DK ** -0.5), ("kg", kg_s, None), ("vg", vg_s, None)):
        projs += [functools.partial(plain, group, dst, scale, c0) for c0 in range(0, dst.shape[2], MXU_COLS)]
    projs += [functools.partial(out_gate, c0) for c0 in range(0, D_GLA, MXU_COLS)] + [forget_gate]
    for n, m in enumerate(mix):
        for p in projs[n * len(projs) // len(mix):(n + 1) * len(projs) // len(mix)]:
            p()
        m()
    kvc_s[...] = kvs_s[rs, tm - WINDOW:tm, :]


def _mixer(x2, w_in, w_gk2, b_gk, norm_w, sinks, seq, w_out, w_up, w_down):
    n = x2.shape[0]
    tm = min(MIX_ROWS, seq)
    chunk = min(GLA_CHUNK, tm)
    n_tiles = n // tm
    bounds = [0]
    for s in SPLITS:
        bounds.append(bounds[-1] + s)
    cols = [w_in[:, bounds[i]:bounds[i + 1]].astype(BF16) for i in range(len(SPLITS))]
    wqg, wkg, wvg, wgg, wgl, wqs, wks, wvs = cols
    wkvs = jnp.concatenate([wks, wvs], axis=1)
    wgl = jnp.pad(wgl, ((0, 0), (0, LANES - GATE_RANK)))
    wgk2 = jnp.pad(w_gk2.astype(BF16), ((0, LANES - GATE_RANK), (0, 0)))
    bgk = b_gk.reshape(1, GLA_KW).astype(F32)
    nw = jnp.tile(norm_w.astype(F32), GLA_HEADS).reshape(1, D_GLA)
    weights = (wqs, wkvs, wqg, wkg, wvg, wgg, wgl, wgk2, bgk, nw)
    slot = lambda d, t: pltpu.VMEM((2, tm, d), t)
    out_spec = pl.BlockSpec((tm, D_GLA), lambda i: (jnp.maximum(i - 1, 0), 0))
    blk = lambda i: jnp.minimum(i, n_tiles - 1)
    assert D_MODEL % n_tiles == 0 and D_FF % n_tiles == 0
    cast_specs = [pl.BlockSpec((D_MODEL // n_tiles, D_MODEL), lambda i: (blk(i), 0)),
                  pl.BlockSpec((D_MODEL, D_FF // n_tiles), lambda i: (0, blk(i))),
                  pl.BlockSpec((D_FF // n_tiles, D_MODEL), lambda i: (blk(i), 0))]
    return pl.pallas_call(
        functools.partial(_mixer_kernel, tiles_per_seq=seq // tm, chunk=chunk),
        grid=(n_tiles + 1,),
        in_specs=[pl.BlockSpec(memory_space=pltpu.SMEM),
                  pl.BlockSpec((tm, D_MODEL), lambda i: (jnp.minimum(i, n_tiles - 1), 0))]
                 + [_resident(w.shape) for w in weights] + cast_specs,
        out_specs=[out_spec, out_spec] + cast_specs,
        out_shape=[jax.ShapeDtypeStruct((n, D_GLA), BF16), jax.ShapeDtypeStruct((n, D_SWA), BF16)]
                  + [jax.ShapeDtypeStruct(w.shape, BF16) for w in (w_out, w_up, w_down)],
        scratch_shapes=[pltpu.VMEM((tm, D_MODEL), BF16),
                        slot(D_SWA, BF16), slot(2 * SWA_KVW, BF16), slot(GLA_KW, BF16), slot(GLA_KW, BF16),
                        slot(D_GLA, BF16), slot(D_GLA, BF16), slot(GLA_KW, F32), slot(GLA_KW, F32),
                        pltpu.VMEM((GLA_HEADS, GLA_DV, GLA_DK), F32),
                        pltpu.VMEM((WINDOW, 2 * SWA_KVW), BF16)],
        compiler_params=_params("arbitrary"),
        name="token_mixer",
    )(sinks.astype(F32), x2, *weights, w_out, w_up, w_down)


def _channel_kernel(gla_ref, swa_ref, x_ref, wo_ref, g1_ref, b1_ref, wu_ref, wd_ref, g2_ref, b2_ref,
                    o_ref, xb_ref):
    f = pl.program_id(1)
    last = pl.num_programs(1) - 1
    tm = x_ref.shape[0]

    def hidden():
        h = jnp.maximum(_nn(xb_ref[...], wu_ref[...]), 0.0)
        return (h * h).astype(BF16)

    @pl.when(f == 0)
    def _():
        for r0 in range(0, tm, LN_SUB_ROWS):
            rows = pl.ds(r0, LN_SUB_ROWS)
            mix = _nn(gla_ref[rows, :], wo_ref[:D_GLA, :]) + _nn(swa_ref[rows, :], wo_ref[D_GLA:, :])
            x1 = _layer_norm(ALPHA * x_ref[rows, :] + mix, g1_ref[...], b1_ref[...])
            xb_ref[rows, :] = x1.astype(BF16)
            o_ref[rows, :] = ALPHA * x1
        o_ref[...] += _nn(hidden(), wd_ref[...])

    @pl.when(jnp.logical_and(f > 0, f < last))
    def _():
        o_ref[...] += _nn(hidden(), wd_ref[...])

    @pl.when(f == last)
    def _():
        h = hidden()
        for r0 in range(0, tm, LN_SUB_ROWS):
            rows = pl.ds(r0, LN_SUB_ROWS)
            y = o_ref[rows, :] + _nn(h[r0:r0 + LN_SUB_ROWS, :], wd_ref[...])
            o_ref[rows, :] = _layer_norm(y, g2_ref[...], b2_ref[...])


def _channel_mixer(gla, swa, x2, w_out, ln1_g, ln1_b, w_up, w_down, ln2_g, ln2_b):
    n = x2.shape[0]
    tm = min(MLP_ROWS, n)
    tf = MLP_FF_TILE
    assert D_FF // tf >= 2
    row = lambda d: pl.BlockSpec((tm, d), lambda i, f: (i, 0))
    vec = lambda: _resident((1, D_MODEL))
    as_row = lambda t: t.reshape(1, D_MODEL)
    return pl.pallas_call(
        _channel_kernel,
        grid=(n // tm, D_FF // tf),
        in_specs=[row(D_GLA), row(D_SWA), row(D_MODEL), _resident(w_out.shape), vec(), vec(),
                  pl.BlockSpec((D_MODEL, tf), lambda i, f: (0, f)),
                  pl.BlockSpec((tf, D_MODEL), lambda i, f: (f, 0)), vec(), vec()],
        out_specs=row(D_MODEL),
        out_shape=jax.ShapeDtypeStruct((n, D_MODEL), F32),
        scratch_shapes=[pltpu.VMEM((tm, D_MODEL), BF16)],
        compiler_params=_params("parallel", "arbitrary"),
        name="channel_mixer",
    )(gla, swa, x2, w_out, as_row(ln1_g), as_row(ln1_b), w_up, w_down,
      as_row(ln2_g), as_row(ln2_b))


def kernel(x, w_in, w_gk2, b_gk, gla_norm_w, swa_sinks, w_out, ln1_g, ln1_b, w_up, w_down, ln2_g, ln2_b):
    bsz, seq, _ = x.shape
    n = bsz * seq
    x2 = x.reshape(n, D_MODEL)
    for l in range(DEPTH):
        gla, swa, wo, wu, wd = _mixer(x2, w_in[l], w_gk2[l], b_gk[l], gla_norm_w[l], swa_sinks[l], seq,
                                      w_out[l], w_up[l], w_down[l])
        x2 = _channel_mixer(gla, swa, x2, wo, ln1_g[l], ln1_b[l], wu, wd, ln2_g[l], ln2_b[l])
    return x2.reshape(bsz, seq, D_MODEL)
```

```python
import functools

import jax
import jax.numpy as jnp
from jax import lax
from jax.experimental import pallas as pl
from jax.experimental.pallas import tpu as pltpu

F32 = jnp.float32
BF16 = jnp.bfloat16

D_MODEL = 2048
D_GLA = 1024
GLA_HEADS = 4
GLA_DK = 128
GLA_DV = 256
GLA_KW = GLA_HEADS * GLA_DK
GATE_RANK = 16
GATE_TAU = 16.0
SWA_HEAD_DIM = 64
SWA_Q_HEADS = 16
SWA_KV_HEADS = 2
SWA_GROUP = SWA_Q_HEADS // SWA_KV_HEADS
D_SWA = SWA_Q_HEADS * SWA_HEAD_DIM
SWA_KVW = SWA_KV_HEADS * SWA_HEAD_DIM
WINDOW = 128
D_FF = 4 * D_MODEL
SPLITS = (GLA_KW, GLA_KW, D_GLA, D_GLA, GATE_RANK, D_SWA, SWA_KVW, SWA_KVW)
DEPTH = 1
ALPHA = (2 * DEPTH) ** 0.25
LN_EPS = 1e-5
RMS_EPS = 1e-5

LANES = 128
SUBLANES = 8
MXU_COLS = 256
VMEM_LIMIT_BYTES = 56 * 1024 * 1024

MIX_ROWS = 256
LN_SUB_ROWS = 128
GLA_CHUNK = 128
MLP_ROWS = 512
MLP_FF_TILE = 1024
NEG_BIG = -1e30
LOG2_E = 1.4426950408889634

assert MIX_ROWS % GLA_CHUNK == 0 and MIX_ROWS % WINDOW == 0

def _nn(a, b):
    return lax.dot_general(a, b, (((1,), (0,)), ((), ())), preferred_element_type=F32)


def _nt(a, b):
    return lax.dot_general(a, b, (((1,), (1,)), ((), ())), preferred_element_type=F32)


def _tn(a, b):
    return lax.dot_general(a, b, (((0,), (0,)), ((), ())), preferred_element_type=F32)


def _layer_norm(y, g, b):
    mu = jnp.mean(y, axis=-1, keepdims=True)
    d = y - mu
    var = jnp.mean(d * d, axis=-1, keepdims=True)
    return d * lax.rsqrt(var + LN_EPS) * g + b


def _params(*semantics):
    return pltpu.CompilerParams(dimension_semantics=semantics, vmem_limit_bytes=VMEM_LIMIT_BYTES)


def _resident(shape):
    nd = len(shape)
    return pl.BlockSpec(shape, lambda *_: (0,) * nd, pipeline_mode=pl.Buffered(1))


W_OFFSET = {"qs": 0, "kvs": D_SWA, "qg": D_SWA + 2 * SWA_KVW}
W_OFFSET["kg"] = W_OFFSET["qg"] + GLA_KW
W_OFFSET["vg"] = W_OFFSET["kg"] + GLA_KW
W_OFFSET["gg"] = W_OFFSET["vg"] + D_GLA
W_OFFSET["gl"] = W_OFFSET["gg"] + D_GLA
W_COLS = W_OFFSET["gl"] + LANES
PREP_ROWS = 256


def _prep_kernel(w_ref, o_ref):
    gla_w = 2 * GLA_KW + 2 * D_GLA
    swa0 = gla_w + GATE_RANK
    swa_w = D_SWA + 2 * SWA_KVW
    o_ref[:, :swa_w] = w_ref[:, swa0:swa0 + swa_w].astype(BF16)
    o_ref[:, swa_w:swa_w + gla_w] = w_ref[:, :gla_w].astype(BF16)
    lane = lax.broadcasted_iota(jnp.int32, (w_ref.shape[0], LANES), 1)
    o_ref[:, swa_w + gla_w:] = jnp.where(lane < GATE_RANK, w_ref[:, gla_w:gla_w + LANES], 0.0).astype(BF16)


def _prep_w_in(w_in):
    rows = w_in.shape[0]
    return pl.pallas_call(
        _prep_kernel,
        grid=(rows // PREP_ROWS,),
        in_specs=[pl.BlockSpec((PREP_ROWS, w_in.shape[1]), lambda i: (i, 0))],
        out_specs=pl.BlockSpec((PREP_ROWS, W_COLS), lambda i: (i, 0)),
        out_shape=jax.ShapeDtypeStruct((rows, W_COLS), BF16),
        compiler_params=_params("parallel"),
        name="w_in_prep",
    )(w_in)


def _gla_chunk_head(q_ref, k_ref, v_ref, b_ref, gk_ref, gate_ref, st_ref, o_ref, slot, base, h, c, consts):
    eye, small, small_mask, row_cd, sub8, lane8 = consts
    dk, dv = GLA_DK, GLA_DV
    rows = pl.ds(base, c)
    kcols = pl.ds(h * dk, dk)
    vcols = pl.ds(h * dv, dv)

    def brow(r):
        return jnp.broadcast_to(b_ref[slot, pl.ds(r, 1), kcols], (SUBLANES, dk))

    b = b_ref[slot, rows, kcols]
    qb = q_ref[slot, rows, kcols]
    kb = k_ref[slot, rows, kcols]
    q = qb.astype(F32)
    k = kb.astype(F32)
    v = v_ref[slot, rows, vcols]
    b_last = b_ref[slot, pl.ds(base + c - 1, 1), kcols]

    attn = jnp.where(eye, _nt(qb, kb), 0.0)
    for s in small:
        upper = (row_cd & s) != 0
        if s == 1:
            expo = jnp.where(upper, gk_ref[slot, rows, kcols], 0.0)
        else:
            tiles = []
            for r0 in range(0, c, SUBLANES):
                mids = [brow(base + r0 + o + s - 1) for o in range(0, SUBLANES, 2 * s)]
                m8 = mids[-1]
                for idx in range(len(mids) - 2, -1, -1):
                    m8 = jnp.where(sub8 < (idx + 1) * 2 * s, mids[idx], m8)
                tiles.append(m8)
            expo = (b - jnp.concatenate(tiles, axis=0)) * jnp.where(upper, 1.0, -1.0)
        t = (jnp.where(upper, q, k) * jnp.exp2(expo)).astype(BF16)
        attn = jnp.where(small_mask[s], _nt(t, t), attn)

    slabs = [attn[r:r + SUBLANES, :] for r in range(0, c, SUBLANES)]
    s = SUBLANES
    while 2 * s <= c:
        sel, expo = [], []
        for r0 in range(0, c, 2 * s):
            mid = jnp.concatenate([brow(base + r0 + s - 1)] * (s // SUBLANES), axis=0)
            sel += [k[r0:r0 + s, :], q[r0 + s:r0 + 2 * s, :]]
            expo += [mid - b[r0:r0 + s, :], b[r0 + s:r0 + 2 * s, :] - mid]
        t = (jnp.concatenate(sel, axis=0) * jnp.exp2(jnp.concatenate(expo, axis=0))).astype(BF16)
        g_lvl = _nt(t, t)
        for r0 in range(0, c, 2 * s):
            in_lower = (lane8 >= r0) & (lane8 < r0 + s)
            for r in range(r0 + s, r0 + 2 * s, SUBLANES):
                slabs[r // SUBLANES] = jnp.where(in_lower, g_lvl[r:r + SUBLANES, :], slabs[r // SUBLANES])
        s *= 2
    attn = jnp.concatenate(slabs, axis=0).astype(BF16)

    st = st_ref[h]
    q_in = (q * jnp.exp2(b)).astype(BF16)
    o = _nt(q_in, st.astype(BF16)) + _nn(attn, v)
    k_out = (k * jnp.exp2(b_last - b)).astype(BF16)
    st_ref[h] = st * jnp.exp2(b_last) + _tn(v, k_out)

    o = o * lax.rsqrt(jnp.mean(o * o, axis=-1, keepdims=True) + RMS_EPS)
    o_ref[rows, vcols] = (o * gate_ref[slot, rows, vcols].astype(F32)).astype(o_ref.dtype)


def _gla_consts(c):
    row_cc = lax.broadcasted_iota(jnp.int32, (c, c), 0)
    col_cc = lax.broadcasted_iota(jnp.int32, (c, c), 1)
    eye = row_cc == col_cc
    small = [s for s in (1, 2, 4) if 2 * s <= min(c, SUBLANES)]
    small_mask = {s: ((row_cc & s) != 0) & ((col_cc & s) == 0) & ((row_cc ^ col_cc) < 2 * s) for s in small}
    row_cd = lax.broadcasted_iota(jnp.int32, (c, GLA_DK), 0)
    sub8 = lax.broadcasted_iota(jnp.int32, (SUBLANES, GLA_DK), 0)
    lane8 = lax.broadcasted_iota(jnp.int32, (SUBLANES, c), 1)
    return eye, small, small_mask, row_cd, sub8, lane8


def _swa_block(sink_ref, q_ref, slot, row0, k_prev, k_cur, v_prev, v_cur, prev_penalty, o_ref):
    w = WINDOW
    hd = SWA_HEAD_DIM
    rows = pl.ds(row0, w)
    kf = jnp.concatenate([k_prev, k_cur], axis=0).astype(F32)
    vf = jnp.concatenate([v_prev, v_cur], axis=0).astype(F32)
    low_kv = lax.broadcasted_iota(jnp.int32, (2 * w, LANES), 1) < hd
    qi = lax.broadcasted_iota(jnp.int32, (w, w), 0)
    kj = lax.broadcasted_iota(jnp.int32, (w, w), 1)
    from_prev = kj > qi
    low_o = lax.broadcasted_iota(jnp.int32, (w, LANES), 1) < hd
    ones_lo = low_kv.astype(BF16)
    ones_hi = (~low_kv).astype(BF16)
    pairs = SWA_GROUP // 2
    for g in range(SWA_KV_HEADS):
        k_rot = kf if g == 0 else pltpu.roll(kf, hd, axis=1)
        v_rot = vf if g == 0 else pltpu.roll(vf, hd, axis=1)
        k_lo = jnp.where(low_kv, k_rot, 0.0)
        v_lo = jnp.where(low_kv, v_rot, 0.0)
        k_hi = pltpu.roll(k_lo, hd, axis=1).astype(BF16)
        v_hi = pltpu.roll(v_lo, hd, axis=1).astype(BF16)
        vw = (jnp.concatenate([v_lo.astype(BF16), ones_lo], axis=1),
              jnp.concatenate([v_hi, ones_hi], axis=1))
        kk = jnp.concatenate([k_lo.astype(BF16), k_hi], axis=0)
        qst = jnp.concatenate(
            [q_ref[slot, rows, pl.ds((g * pairs + p) * LANES, LANES)] for p in range(pairs)], axis=0)
        s_all = _nt(qst, kk)
        acc = None
        sink_terms = []
        for par in range(2):
            probs, terms = [], []
            for p in range(pairs):
                sink = sink_ref[2 * (g * pairs + p) + par] * LOG2_E
                s2 = s_all[p * w:(p + 1) * w, 2 * w * par:2 * w * (par + 1)]
                s_prev = s2[:, :w] if prev_penalty is None else s2[:, :w] + prev_penalty
                s = jnp.where(from_prev, s_prev, s2[:, w:])
                m = jnp.maximum(jnp.max(s, axis=-1, keepdims=True), sink)
                e = jnp.exp2(s - m)
                probs.append(jnp.concatenate([jnp.where(from_prev, e, 0.0).astype(BF16),
                                              jnp.where(from_prev, 0.0, e).astype(BF16)], axis=1))
                terms.append(jnp.exp2(sink - m))
            part = _nn(jnp.concatenate(probs, axis=0), vw[par])
            acc = part if acc is None else acc + part
            sink_terms.append(terms)
        for p in range(pairs):
            blk = acc[p * w:(p + 1) * w, :]
            denom = blk[:, LANES:] + jnp.where(low_o, sink_terms[0][p], sink_terms[1][p])
            o_ref[rows, pl.ds((g * pairs + p) * LANES, LANES)] = (blk[:, :LANES] / denom).astype(o_ref.dtype)


def _mixer_kernel(sink_ref, x_ref, w_ref, wgk2, bgk, nw, wo_f, wu_f, wd_f,
                  gla_o, swa_o, wo_b, wu_b, wd_b,
                  xb_s, qs_s, kvs_s, qg_s, kg_s, vg_s, gate_s, gk_s, b_s, st_s, kvc_s,
                  *, tiles_per_seq, chunk):
    for src, dst in ((wo_f, wo_b), (wu_f, wu_b), (wd_f, wd_b)):
        dst[...] = src[...].astype(BF16)

    i = pl.program_id(0)
    ws = lax.rem(i, 2)
    rs = 1 - ws
    seq_start = lax.rem(i + tiles_per_seq - 1, tiles_per_seq) == 0
    tm = x_ref.shape[0]
    proj_scratch = (qs_s, kvs_s, qg_s, kg_s, vg_s, gate_s, gk_s, b_s)

    @pl.when(i == 0)
    def _():
        for s in proj_scratch:
            s[1] = jnp.zeros(s.shape[1:], s.dtype)
        kvc_s[...] = jnp.zeros_like(kvc_s)

    @pl.when(jnp.logical_or(i == 0, seq_start))
    def _():
        st_s[...] = jnp.zeros_like(st_s)

    consts = _gla_consts(chunk)
    row = lax.broadcasted_iota(jnp.int32, (chunk, chunk), 0)
    col = lax.broadcasted_iota(jnp.int32, (chunk, chunk), 1)
    tril = (col <= row).astype(BF16)

    def proj(group, c0, width):
        w0 = W_OFFSET[group] + c0
        return _nn(xb_s[...], w_ref[:, w0:w0 + width])

    def plain(group, dst, scale, c0):
        y = proj(group, c0, MXU_COLS)
        dst[ws, :, c0:c0 + MXU_COLS] = (y if scale is None else y * scale).astype(BF16)

    def out_gate(c0):
        g = proj("gg", c0, MXU_COLS)
        gate_s[ws, :, c0:c0 + MXU_COLS] = (nw[:, c0:c0 + MXU_COLS] * g / (1.0 + jnp.exp(-g))).astype(BF16)

    def forget_gate():
        gl = proj("gl", 0, LANES).astype(BF16)
        z = _nn(gl, wgk2[...]) + bgk[...]
        gk = (jnp.minimum(z, 0.0) - jnp.log1p(jnp.exp(-jnp.abs(z)))) * (LOG2_E / GATE_TAU)
        gk_s[ws] = gk
        gk_hi = gk.astype(BF16)
        gk_lo = (gk - gk_hi.astype(F32)).astype(BF16)
        for r0 in range(0, tm, chunk):
            b_s[ws, r0:r0 + chunk, :] = (_nn(tril, gk_hi[r0:r0 + chunk, :]) + _nn(tril, gk_lo[r0:r0 + chunk, :]))

    def swa(j):
        r0 = j * WINDOW
        if j == 0:
            k_prev, v_prev = kvc_s[:, :SWA_KVW], kvc_s[:, SWA_KVW:]
            penalty = jnp.where(seq_start, NEG_BIG, 0.0)
        else:
            k_prev = kvs_s[rs, r0 - WINDOW:r0, :SWA_KVW]
            v_prev = kvs_s[rs, r0 - WINDOW:r0, SWA_KVW:]
            penalty = None
        _swa_block(sink_ref, qs_s, rs, r0, k_prev, kvs_s[rs, r0:r0 + WINDOW, :SWA_KVW],
                   v_prev, kvs_s[rs, r0:r0 + WINDOW, SWA_KVW:], penalty, swa_o)

    def gla(ci):
        for h in range(GLA_HEADS):
            _gla_chunk_head(qg_s, kg_s, vg_s, b_s, gk_s, gate_s, st_s, gla_o, rs, ci * chunk, h, chunk, consts)

    xb_s[...] = x_ref[...].astype(BF16)
    n_swa = tm // WINDOW
    n_gla = tm // chunk
    mix = []
    for n in range(max(n_swa, n_gla)):
        mix += [functools.partial(swa, n)] if n < n_swa else []
        mix += [functools.partial(gla, n)] if n < n_gla else []
    projs = []
    for group, dst, scale in (("qs", qs_s, SWA_HEAD_DIM ** -0.5 * LOG2_E), ("kvs", kvs_s, None),
                              ("qg", qg_s, GLA_DK ** -0.5), ("kg", kg_s, None), ("vg", vg_s, None)):
        projs += [functools.partial(plain, group, dst, scale, c0) for c0 in range(0, dst.shape[2], MXU_COLS)]
    projs += [functools.partial(out_gate, c0) for c0 in range(0, D_GLA, MXU_COLS)] + [forget_gate]
    for n, m in enumerate(mix):
        for p in projs[n * len(projs) // len(mix):(n + 1) * len(projs) // len(mix)]:
            p()
        m()
    kvc_s[...] = kvs_s[rs, tm - WINDOW:tm, :]


def _mixer(x2, w_in, w_gk2, b_gk, norm_w, sinks, seq, w_out, w_up, w_down):
    n = x2.shape[0]
    tm = min(MIX_ROWS, seq)
    chunk = min(GLA_CHUNK, tm)
    n_tiles = n // tm
    wgk2 = jnp.pad(w_gk2.astype(BF16), ((0, LANES - GATE_RANK), (0, 0)))
    bgk = b_gk.reshape(1, GLA_KW).astype(F32)
    nw = jnp.tile(norm_w.astype(F32), GLA_HEADS).reshape(1, D_GLA)
    weights = (_prep_w_in(w_in), wgk2, bgk, nw)
    slot = lambda d, t: pltpu.VMEM((2, tm, d), t)
    out_spec = pl.BlockSpec((tm, D_GLA), lambda i: (jnp.maximum(i - 1, 0), 0))
    blk = lambda i: jnp.minimum(i, n_tiles - 1)
    assert D_MODEL % n_tiles == 0 and D_FF % n_tiles == 0
    cast_specs = [pl.BlockSpec((D_MODEL // n_tiles, D_MODEL), lambda i: (blk(i), 0)),
                  pl.BlockSpec((D_MODEL, D_FF // n_tiles), lambda i: (0, blk(i))),
                  pl.BlockSpec((D_FF // n_tiles, D_MODEL), lambda i: (blk(i), 0))]
    return pl.pallas_call(
        functools.partial(_mixer_kernel, tiles_per_seq=seq // tm, chunk=chunk),
        grid=(n_tiles + 1,),
        in_specs=[pl.BlockSpec(memory_space=pltpu.SMEM),
                  pl.BlockSpec((tm, D_MODEL), lambda i: (jnp.minimum(i, n_tiles - 1), 0))]
                 + [_resident(w.shape) for w in weights] + cast_specs,
        out_specs=[out_spec, out_spec] + cast_specs,
        out_shape=[jax.ShapeDtypeStruct((n, D_GLA), BF16), jax.ShapeDtypeStruct((n, D_SWA), BF16)]
                  + [jax.ShapeDtypeStruct(w.shape, BF16) for w in (w_out, w_up, w_down)],
        scratch_shapes=[pltpu.VMEM((tm, D_MODEL), BF16),
                        slot(D_SWA, BF16), slot(2 * SWA_KVW, BF16), slot(GLA_KW, BF16), slot(GLA_KW, BF16),
                        slot(D_GLA, BF16), slot(D_GLA, BF16), slot(GLA_KW, F32), slot(GLA_KW, F32),
                        pltpu.VMEM((GLA_HEADS, GLA_DV, GLA_DK), F32),
                        pltpu.VMEM((WINDOW, 2 * SWA_KVW), BF16)],
        compiler_params=_params("arbitrary"),
        name="token_mixer",
    )(sinks.astype(F32), x2, *weights, w_out, w_up, w_down)


def _channel_kernel(gla_ref, swa_ref, x_ref, wo_ref, g1_ref, b1_ref, wu_ref, wd_ref, g2_ref, b2_ref,
                    o_ref, xb_ref):
    f = pl.program_id(1)
    last = pl.num_programs(1) - 1
    tm = x_ref.shape[0]

    def hidden():
        h = jnp.maximum(_nn(xb_ref[...], wu_ref[...]), 0.0)
        return (h * h).astype(BF16)

    subs = [pl.ds(r0, LN_SUB_ROWS) for r0 in range(0, tm, LN_SUB_ROWS)]

    @pl.when(f == 0)
    def _():
        for n in range(len(subs) + 1):
            if n < len(subs):
                rows = subs[n]
                o_ref[rows, :] = (ALPHA * x_ref[rows, :] + _nn(gla_ref[rows, :], wo_ref[:D_GLA, :])
                                  + _nn(swa_ref[rows, :], wo_ref[D_GLA:, :]))
            if n >= 1:
                rows = subs[n - 1]
                x1 = _layer_norm(o_ref[rows, :], g1_ref[...], b1_ref[...])
                xb_ref[rows, :] = x1.astype(BF16)
                o_ref[rows, :] = ALPHA * x1
        o_ref[...] += _nn(hidden(), wd_ref[...])

    @pl.when(jnp.logical_and(f > 0, f < last))
    def _():
        o_ref[...] += _nn(hidden(), wd_ref[...])

    @pl.when(f == last)
    def _():
        h = hidden()
        for n in range(len(subs) + 1):
            if n < len(subs):
                o_ref[subs[n], :] += _nn(h[n * LN_SUB_ROWS:(n + 1) * LN_SUB_ROWS, :], wd_ref[...])
            if n >= 1:
                rows = subs[n - 1]
                o_ref[rows, :] = _layer_norm(o_ref[rows, :], g2_ref[...], b2_ref[...])


def _channel_mixer(gla, swa, x2, w_out, ln1_g, ln1_b, w_up, w_down, ln2_g, ln2_b):
    n = x2.shape[0]
    tm = min(MLP_ROWS, n)
    tf = MLP_FF_TILE
    assert D_FF // tf >= 2
    row = lambda d: pl.BlockSpec((tm, d), lambda i, f: (i, 0))
    vec = lambda: _resident((1, D_MODEL))
    as_row = lambda t: t.reshape(1, D_MODEL)
    return pl.pallas_call(
        _channel_kernel,
        grid=(n // tm, D_FF // tf),
        in_specs=[row(D_GLA), row(D_SWA), row(D_MODEL), _resident(w_out.shape), vec(), vec(),
                  pl.BlockSpec((D_MODEL, tf), lambda i, f: (0, f)),
                  pl.BlockSpec((tf, D_MODEL), lambda i, f: (f, 0)), vec(), vec()],
        out_specs=row(D_MODEL),
        out_shape=jax.ShapeDtypeStruct((n, D_MODEL), F32),
        scratch_shapes=[pltpu.VMEM((tm, D_MODEL), BF16)],
        compiler_params=_params("parallel", "arbitrary"),
        name="channel_mixer",
    )(gla, swa, x2, w_out, as_row(ln1_g), as_row(ln1_b), w_up, w_down,
      as_row(ln2_g), as_row(ln2_b))


def kernel(x, w_in, w_gk2, b_gk, gla_norm_w, swa_sinks, w_out, ln1_g, ln1_b, w_up, w_down, ln2_g, ln2_b):
    bsz, seq, _ = x.shape
    n = bsz * seq
    x2 = x.reshape(n, D_MODEL)
    for l in range(DEPTH):
        gla, swa, wo, wu, wd = _mixer(x2, w_in[l], w_gk2[l], b_gk[l], gla_norm_w[l], swa_sinks[l], seq,
                                      w_out[l], w_up[l], w_down[l])
        x2 = _channel_mixer(gla, swa, x2, wo, ln1_g[l], ln1_b[l], wu, wd, ln2_g[l], ln2_b[l])
    return x2.reshape(bsz, seq, D_MODEL)
```

```python
import functools

import jax
import jax.numpy as jnp
from jax import lax
from jax.experimental import pallas as pl
from jax.experimental.pallas import tpu as pltpu

F32 = jnp.float32
BF16 = jnp.bfloat16

D_MODEL = 2048
D_GLA = 1024
GLA_HEADS = 4
GLA_DK = 128
GLA_DV = 256
GLA_KW = GLA_HEADS * GLA_DK
GATE_RANK = 16
GATE_TAU = 16.0
SWA_HEAD_DIM = 64
SWA_Q_HEADS = 16
SWA_KV_HEADS = 2
SWA_GROUP = SWA_Q_HEADS // SWA_KV_HEADS
D_SWA = SWA_Q_HEADS * SWA_HEAD_DIM
SWA_KVW = SWA_KV_HEADS * SWA_HEAD_DIM
WINDOW = 128
D_FF = 4 * D_MODEL
SPLITS = (GLA_KW, GLA_KW, D_GLA, D_GLA, GATE_RANK, D_SWA, SWA_KVW, SWA_KVW)
DEPTH = 1
ALPHA = (2 * DEPTH) ** 0.25
LN_EPS = 1e-5
RMS_EPS = 1e-5

LANES = 128
SUBLANES = 8
MXU_COLS = 256
VMEM_LIMIT_BYTES = 56 * 1024 * 1024

MIX_ROWS = 256
LN_SUB_ROWS = 128
GLA_CHUNK = 128
MLP_ROWS = 512
MLP_FF_TILE = 1024
NEG_BIG = -1e30
LOG2_E = 1.4426950408889634

assert MIX_ROWS % GLA_CHUNK == 0 and MIX_ROWS % WINDOW == 0

def _nn(a, b):
    return lax.dot_general(a, b, (((1,), (0,)), ((), ())), preferred_element_type=F32)


def _nt(a, b):
    return lax.dot_general(a, b, (((1,), (1,)), ((), ())), preferred_element_type=F32)


def _tn(a, b):
    return lax.dot_general(a, b, (((0,), (0,)), ((), ())), preferred_element_type=F32)


def _layer_norm(y, g, b):
    mu = jnp.mean(y, axis=-1, keepdims=True)
    d = y - mu
    var = jnp.mean(d * d, axis=-1, keepdims=True)
    return d * lax.rsqrt(var + LN_EPS) * g + b


def _params(*semantics):
    return pltpu.CompilerParams(dimension_semantics=semantics, vmem_limit_bytes=VMEM_LIMIT_BYTES)


def _resident(shape):
    nd = len(shape)
    return pl.BlockSpec(shape, lambda *_: (0,) * nd, pipeline_mode=pl.Buffered(1))


W_OFFSET = {"qs": 0, "kvs": D_SWA, "qg": D_SWA + 2 * SWA_KVW}
W_OFFSET["kg"] = W_OFFSET["qg"] + GLA_KW
W_OFFSET["vg"] = W_OFFSET["kg"] + GLA_KW
W_OFFSET["gg"] = W_OFFSET["vg"] + D_GLA
W_OFFSET["gl"] = W_OFFSET["gg"] + D_GLA
W_COLS = W_OFFSET["gl"] + LANES
PREP_ROWS = 256


def _prep_kernel(wt_ref, o_ref):
    gla_w = 2 * GLA_KW + 2 * D_GLA
    swa0 = gla_w + GATE_RANK
    swa_w = D_SWA + 2 * SWA_KVW
    o_ref[:, :swa_w] = wt_ref[swa0:swa0 + swa_w, :].T.astype(BF16)
    o_ref[:, swa_w:swa_w + gla_w] = wt_ref[:gla_w, :].T.astype(BF16)
    gate_rank = jnp.concatenate([wt_ref[gla_w:swa0, :], jnp.zeros((LANES - GATE_RANK, wt_ref.shape[1]), F32)], axis=0)
    o_ref[:, swa_w + gla_w:] = gate_rank.T.astype(BF16)


def _prep_w_in(w_in):
    rows, feats = w_in.shape
    return pl.pallas_call(
        _prep_kernel,
        grid=(rows // PREP_ROWS,),
        in_specs=[pl.BlockSpec((feats, PREP_ROWS), lambda i: (0, i))],
        out_specs=pl.BlockSpec((PREP_ROWS, W_COLS), lambda i: (i, 0)),
        out_shape=jax.ShapeDtypeStruct((rows, W_COLS), BF16),
        compiler_params=_params("parallel"),
        name="w_in_prep",
    )(w_in.T)


def _gla_chunk_head(q_ref, k_ref, v_ref, b_ref, gk_ref, gate_ref, st_ref, o_ref, slot, base, h, c, consts):
    eye, small, small_mask, row_cd, sub8, lane8 = consts
    dk, dv = GLA_DK, GLA_DV
    rows = pl.ds(base, c)
    kcols = pl.ds(h * dk, dk)
    vcols = pl.ds(h * dv, dv)

    def brow(r):
        return jnp.broadcast_to(b_ref[slot, pl.ds(r, 1), kcols], (SUBLANES, dk))

    b = b_ref[slot, rows, kcols]
    qb = q_ref[slot, rows, kcols]
    kb = k_ref[slot, rows, kcols]
    q = qb.astype(F32)
    k = kb.astype(F32)
    v = v_ref[slot, rows, vcols]
    b_last = b_ref[slot, pl.ds(base + c - 1, 1), kcols]

    attn = jnp.where(eye, _nt(qb, kb), 0.0)
    for s in small:
        upper = (row_cd & s) != 0
        if s == 1:
            expo = jnp.where(upper, gk_ref[slot, rows, kcols], 0.0)
        else:
            tiles = []
            for r0 in range(0, c, SUBLANES):
                mids = [brow(base + r0 + o + s - 1) for o in range(0, SUBLANES, 2 * s)]
                m8 = mids[-1]
                for idx in range(len(mids) - 2, -1, -1):
                    m8 = jnp.where(sub8 < (idx + 1) * 2 * s, mids[idx], m8)
                tiles.append(m8)
            expo = (b - jnp.concatenate(tiles, axis=0)) * jnp.where(upper, 1.0, -1.0)
        t = (jnp.where(upper, q, k) * jnp.exp2(expo)).astype(BF16)
        attn = jnp.where(small_mask[s], _nt(t, t), attn)

    slabs = [attn[r:r + SUBLANES, :] for r in range(0, c, SUBLANES)]
    s = SUBLANES
    while 2 * s <= c:
        sel, expo = [], []
        for r0 in range(0, c, 2 * s):
            mid = jnp.concatenate([brow(base + r0 + s - 1)] * (s // SUBLANES), axis=0)
            sel += [k[r0:r0 + s, :], q[r0 + s:r0 + 2 * s, :]]
            expo += [mid - b[r0:r0 + s, :], b[r0 + s:r0 + 2 * s, :] - mid]
        t = (jnp.concatenate(sel, axis=0) * jnp.exp2(jnp.concatenate(expo, axis=0))).astype(BF16)
        g_lvl = _nt(t, t)
        for r0 in range(0, c, 2 * s):
            in_lower = (lane8 >= r0) & (lane8 < r0 + s)
            for r in range(r0 + s, r0 + 2 * s, SUBLANES):
                slabs[r // SUBLANES] = jnp.where(in_lower, g_lvl[r:r + SUBLANES, :], slabs[r // SUBLANES])
        s *= 2
    attn = jnp.concatenate(slabs, axis=0).astype(BF16)

    st = st_ref[h]
    q_in = (q * jnp.exp2(b)).astype(BF16)
    o = _nt(q_in, st.astype(BF16)) + _nn(attn, v)
    k_out = (k * jnp.exp2(b_last - b)).astype(BF16)
    st_ref[h] = st * jnp.exp2(b_last) + _tn(v, k_out)

    o = o * lax.rsqrt(jnp.mean(o * o, axis=-1, keepdims=True) + RMS_EPS)
    o_ref[rows, vcols] = (o * gate_ref[slot, rows, vcols].astype(F32)).astype(o_ref.dtype)


def _gla_consts(c):
    row_cc = lax.broadcasted_iota(jnp.int32, (c, c), 0)
    col_cc = lax.broadcasted_iota(jnp.int32, (c, c), 1)
    eye = row_cc == col_cc
    small = [s for s in (1, 2, 4) if 2 * s <= min(c, SUBLANES)]
    small_mask = {s: ((row_cc & s) != 0) & ((col_cc & s) == 0) & ((row_cc ^ col_cc) < 2 * s) for s in small}
    row_cd = lax.broadcasted_iota(jnp.int32, (c, GLA_DK), 0)
    sub8 = lax.broadcasted_iota(jnp.int32, (SUBLANES, GLA_DK), 0)
    lane8 = lax.broadcasted_iota(jnp.int32, (SUBLANES, c), 1)
    return eye, small, small_mask, row_cd, sub8, lane8


def _swa_block(sink_ref, q_ref, slot, row0, k_prev, k_cur, v_prev, v_cur, prev_penalty, o_ref):
    w = WINDOW
    hd = SWA_HEAD_DIM
    rows = pl.ds(row0, w)
    kf = jnp.concatenate([k_prev, k_cur], axis=0).astype(F32)
    vf = jnp.concatenate([v_prev, v_cur], axis=0).astype(F32)
    low_kv = lax.broadcasted_iota(jnp.int32, (2 * w, LANES), 1) < hd
    qi = lax.broadcasted_iota(jnp.int32, (w, w), 0)
    kj = lax.broadcasted_iota(jnp.int32, (w, w), 1)
    from_prev = kj > qi
    low_o = lax.broadcasted_iota(jnp.int32, (w, LANES), 1) < hd
    ones_lo = low_kv.astype(BF16)
    ones_hi = (~low_kv).astype(BF16)
    pairs = SWA_GROUP // 2
    for g in range(SWA_KV_HEADS):
        k_rot = kf if g == 0 else pltpu.roll(kf, hd, axis=1)
        v_rot = vf if g == 0 else pltpu.roll(vf, hd, axis=1)
        k_lo = jnp.where(low_kv, k_rot, 0.0)
        v_lo = jnp.where(low_kv, v_rot, 0.0)
        k_hi = pltpu.roll(k_lo, hd, axis=1).astype(BF16)
        v_hi = pltpu.roll(v_lo, hd, axis=1).astype(BF16)
        vw = (jnp.concatenate([v_lo.astype(BF16), ones_lo], axis=1),
              jnp.concatenate([v_hi, ones_hi], axis=1))
        kk = jnp.concatenate([k_lo.astype(BF16), k_hi], axis=0)
        qst = jnp.concatenate(
            [q_ref[slot, rows, pl.ds((g * pairs + p) * LANES, LANES)] for p in range(pairs)], axis=0)
        s_all = _nt(qst, kk)
        acc = None
        sink_terms = []
        for par in range(2):
            probs, terms = [], []
            for p in range(pairs):
                sink = sink_ref[2 * (g * pairs + p) + par] * LOG2_E
                s2 = s_all[p * w:(p + 1) * w, 2 * w * par:2 * w * (par + 1)]
                s_prev = s2[:, :w] if prev_penalty is None else s2[:, :w] + prev_penalty
                s = jnp.where(from_prev, s_prev, s2[:, w:])
                m = jnp.maximum(jnp.max(s, axis=-1, keepdims=True), sink)
                e = jnp.exp2(s - m)
                probs.append(jnp.concatenate([jnp.where(from_prev, e, 0.0).astype(BF16),
                                              jnp.where(from_prev, 0.0, e).astype(BF16)], axis=1))
                terms.append(jnp.exp2(sink - m))
            part = _nn(jnp.concatenate(probs, axis=0), vw[par])
            acc = part if acc is None else acc + part
            sink_terms.append(terms)
        for p in range(pairs):
            blk = acc[p * w:(p + 1) * w, :]
            denom = blk[:, LANES:] + jnp.where(low_o, sink_terms[0][p], sink_terms[1][p])
            o_ref[rows, pl.ds((g * pairs + p) * LANES, LANES)] = (blk[:, :LANES] / denom).astype(o_ref.dtype)


def _mixer_kernel(sink_ref, x_ref, w_ref, wgk2, bgk, nw, wo_f, wu_f, wd_f,
                  gla_o, swa_o, wo_b, wu_b, wd_b,
                  xb_s, qs_s, kvs_s, qg_s, kg_s, vg_s, gate_s, gk_s, b_s, st_s, kvc_s,
                  *, tiles_per_seq, chunk):
    for src, dst in ((wo_f, wo_b), (wu_f, wu_b), (wd_f, wd_b)):
        dst[...] = src[...].astype(BF16)

    i = pl.program_id(0)
    ws = lax.rem(i, 2)
    rs = 1 - ws
    seq_start = lax.rem(i + tiles_per_seq - 1, tiles_per_seq) == 0
    tm = x_ref.shape[0]
    proj_scratch = (qs_s, kvs_s, qg_s, kg_s, vg_s, gate_s, gk_s, b_s)

    @pl.when(i == 0)
    def _():
        for s in proj_scratch:
            s[1] = jnp.zeros(s.shape[1:], s.dtype)
        kvc_s[...] = jnp.zeros_like(kvc_s)

    @pl.when(jnp.logical_or(i == 0, seq_start))
    def _():
        st_s[...] = jnp.zeros_like(st_s)

    consts = _gla_consts(chunk)
    row = lax.broadcasted_iota(jnp.int32, (chunk, chunk), 0)
    col = lax.broadcasted_iota(jnp.int32, (chunk, chunk), 1)
    tril = (col <= row).astype(BF16)

    def proj(group, c0, width):
        w0 = W_OFFSET[group] + c0
        return _nn(xb_s[...], w_ref[:, w0:w0 + width])

    def plain(group, dst, scale, c0):
        y = proj(group, c0, MXU_COLS)
        dst[ws, :, c0:c0 + MXU_COLS] = (y if scale is None else y * scale).astype(BF16)

    def out_gate(c0):
        g = proj("gg", c0, MXU_COLS)
        gate_s[ws, :, c0:c0 + MXU_COLS] = (nw[:, c0:c0 + MXU_COLS] * g / (1.0 + jnp.exp(-g))).astype(BF16)

    def forget_gate():
        gl = proj("gl", 0, LANES).astype(BF16)
        z = _nn(gl, wgk2[...]) + bgk[...]
        gk = (jnp.minimum(z, 0.0) - jnp.log1p(jnp.exp(-jnp.abs(z)))) * (LOG2_E / GATE_TAU)
        gk_s[ws] = gk
        gk_hi = gk.astype(BF16)
        gk_lo = (gk - gk_hi.astype(F32)).astype(BF16)
        for r0 in range(0, tm, chunk):
            b_s[ws, r0:r0 + chunk, :] = (_nn(tril, gk_hi[r0:r0 + chunk, :]) + _nn(tril, gk_lo[r0:r0 + chunk, :]))

    def swa(j):
        r0 = j * WINDOW
        if j == 0:
            k_prev, v_prev = kvc_s[:, :SWA_KVW], kvc_s[:, SWA_KVW:]
            penalty = jnp.where(seq_start, NEG_BIG, 0.0)
        else:
            k_prev = kvs_s[rs, r0 - WINDOW:r0, :SWA_KVW]
            v_prev = kvs_s[rs, r0 - WINDOW:r0, SWA_KVW:]
            penalty = None
        _swa_block(sink_ref, qs_s, rs, r0, k_prev, kvs_s[rs, r0:r0 + WINDOW, :SWA_KVW],
                   v_prev, kvs_s[rs, r0:r0 + WINDOW, SWA_KVW:], penalty, swa_o)

    def gla(ci):
        for h in range(GLA_HEADS):
            _gla_chunk_head(qg_s, kg_s, vg_s, b_s, gk_s, gate_s, st_s, gla_o, rs, ci * chunk, h, chunk, consts)

    xb_s[...] = x_ref[...].astype(BF16)
    n_swa = tm // WINDOW
    n_gla = tm // chunk
    mix = []
    for n in range(max(n_swa, n_gla)):
        mix += [functools.partial(swa, n)] if n < n_swa else []
        mix += [functools.partial(gla, n)] if n < n_gla else []
    projs = []
    for group, dst, scale in (("qs", qs_s, SWA_HEAD_DIM ** -0.5 * LOG2_E), ("kvs", kvs_s, None),
                              ("qg", qg_s, GLA_DK ** -0.5), ("kg", kg_s, None), ("vg", vg_s, None)):
        projs += [functools.partial(plain, group, dst, scale, c0) for c0 in range(0, dst.shape[2], MXU_COLS)]
    projs += [functools.partial(out_gate, c0) for c0 in range(0, D_GLA, MXU_COLS)] + [forget_gate]
    for n, m in enumerate(mix):
        for p in projs[n * len(projs) // len(mix):(n + 1) * len(projs) // len(mix)]:
            p()
        m()
    kvc_s[...] = kvs_s[rs, tm - WINDOW:tm, :]


def _mixer(x2, w_in, w_gk2, b_gk, norm_w, sinks, seq, w_out, w_up, w_down):
    n = x2.shape[0]
    tm = min(MIX_ROWS, seq)
    chunk = min(GLA_CHUNK, tm)
    n_tiles = n // tm
    wgk2 = jnp.pad(w_gk2.astype(BF16), ((0, LANES - GATE_RANK), (0, 0)))
    bgk = b_gk.reshape(1, GLA_KW).astype(F32)
    nw = jnp.tile(norm_w.astype(F32), GLA_HEADS).reshape(1, D_GLA)
    weights = (_prep_w_in(w_in), wgk2, bgk, nw)
    slot = lambda d, t: pltpu.VMEM((2, tm, d), t)
    out_spec = pl.BlockSpec((tm, D_GLA), lambda i: (jnp.maximum(i - 1, 0), 0))
    blk = lambda i: jnp.minimum(i, n_tiles - 1)
    assert D_MODEL % n_tiles == 0 and D_FF % n_tiles == 0
    cast_specs = [pl.BlockSpec((D_MODEL // n_tiles, D_MODEL), lambda i: (blk(i), 0)),
                  pl.BlockSpec((D_MODEL, D_FF // n_tiles), lambda i: (0, blk(i))),
                  pl.BlockSpec((D_FF // n_tiles, D_MODEL), lambda i: (blk(i), 0))]
    return pl.pallas_call(
        functools.partial(_mixer_kernel, tiles_per_seq=seq // tm, chunk=chunk),
        grid=(n_tiles + 1,),
        in_specs=[pl.BlockSpec(memory_space=pltpu.SMEM),
                  pl.BlockSpec((tm, D_MODEL), lambda i: (jnp.minimum(i, n_tiles - 1), 0))]
                 + [_resident(w.shape) for w in weights] + cast_specs,
        out_specs=[out_spec, out_spec] + cast_specs,
        out_shape=[jax.ShapeDtypeStruct((n, D_GLA), BF16), jax.ShapeDtypeStruct((n, D_SWA), BF16)]
                  + [jax.ShapeDtypeStruct(w.shape, BF16) for w in (w_out, w_up, w_down)],
        scratch_shapes=[pltpu.VMEM((tm, D_MODEL), BF16),
                        slot(D_SWA, BF16), slot(2 * SWA_KVW, BF16), slot(GLA_KW, BF16), slot(GLA_KW, BF16),
                        slot(D_GLA, BF16), slot(D_GLA, BF16), slot(GLA_KW, F32), slot(GLA_KW, F32),
                        pltpu.VMEM((GLA_HEADS, GLA_DV, GLA_DK), F32),
                        pltpu.VMEM((WINDOW, 2 * SWA_KVW), BF16)],
        compiler_params=_params("arbitrary"),
        name="token_mixer",
    )(sinks.astype(F32), x2, *weights, w_out, w_up, w_down)


def _channel_kernel(gla_ref, swa_ref, x_ref, wo_ref, g1_ref, b1_ref, wu_ref, wd_ref, g2_ref, b2_ref,
                    o_ref, xb_ref):
    f = pl.program_id(1)
    last = pl.num_programs(1) - 1
    tm = x_ref.shape[0]

    def hidden():
        h = jnp.maximum(_nn(xb_ref[...], wu_ref[...]), 0.0)
        return (h * h).astype(BF16)

    subs = [pl.ds(r0, LN_SUB_ROWS) for r0 in range(0, tm, LN_SUB_ROWS)]

    @pl.when(f == 0)
    def _():
        for n in range(len(subs) + 1):
            if n < len(subs):
                rows = subs[n]
                o_ref[rows, :] = (ALPHA * x_ref[rows, :] + _nn(gla_ref[rows, :], wo_ref[:D_GLA, :])
                                  + _nn(swa_ref[rows, :], wo_ref[D_GLA:, :]))
            if n >= 1:
                rows = subs[n - 1]
                x1 = _layer_norm(o_ref[rows, :], g1_ref[...], b1_ref[...])
                xb_ref[rows, :] = x1.astype(BF16)
                o_ref[rows, :] = ALPHA * x1
        o_ref[...] += _nn(hidden(), wd_ref[...])

    @pl.when(jnp.logical_and(f > 0, f < last))
    def _():
        o_ref[...] += _nn(hidden(), wd_ref[...])

    @pl.when(f == last)
    def _():
        h = hidden()
        for n in range(len(subs) + 1):
            if n < len(subs):
                o_ref[subs[n], :] += _nn(h[n * LN_SUB_ROWS:(n + 1) * LN_SUB_ROWS, :], wd_ref[...])
            if n >= 1:
                rows = subs[n - 1]
                o_ref[rows, :] = _layer_norm(o_ref[rows, :], g2_ref[...], b2_ref[...])


def _channel_mixer(gla, swa, x2, w_out, ln1_g, ln1_b, w_up, w_down, ln2_g, ln2_b):
    n = x2.shape[0]
    tm = min(MLP_ROWS, n)
    tf = MLP_FF_TILE
    assert D_FF // tf >= 2
    row = lambda d: pl.BlockSpec((tm, d), lambda i, f: (i, 0))
    vec = lambda: _resident((1, D_MODEL))
    as_row = lambda t: t.reshape(1, D_MODEL)
    return pl.pallas_call(
        _channel_kernel,
        grid=(n // tm, D_FF // tf),
        in_specs=[row(D_GLA), row(D_SWA), row(D_MODEL), _resident(w_out.shape), vec(), vec(),
                  pl.BlockSpec((D_MODEL, tf), lambda i, f: (0, f)),
                  pl.BlockSpec((tf, D_MODEL), lambda i, f: (f, 0)), vec(), vec()],
        out_specs=row(D_MODEL),
        out_shape=jax.ShapeDtypeStruct((n, D_MODEL), F32),
        scratch_shapes=[pltpu.VMEM((tm, D_MODEL), BF16)],
        compiler_params=_params("parallel", "arbitrary"),
        name="channel_mixer",
    )(gla, swa, x2, w_out, as_row(ln1_g), as_row(ln1_b), w_up, w_down,
      as_row(ln2_g), as_row(ln2_b))


def kernel(x, w_in, w_gk2, b_gk, gla_norm_w, swa_sinks, w_out, ln1_g, ln1_b, w_up, w_down, ln2_g, ln2_b):
    bsz, seq, _ = x.shape
    n = bsz * seq
    x2 = x.reshape(n, D_MODEL)
    for l in range(DEPTH):
        gla, swa, wo, wu, wd = _mixer(x2, w_in[l], w_gk2[l], b_gk[l], gla_norm_w[l], swa_sinks[l], seq,
                                      w_out[l], w_up[l], w_down[l])
        x2 = _channel_mixer(gla, swa, x2, wo, ln1_g[l], ln1_b[l], wu, wd, ln2_g[l], ln2_b[l])
    return x2.reshape(bsz, seq, D_MODEL)
```

```python
import functools

import jax
import jax.numpy as jnp
from jax import lax
from jax.experimental import pallas as pl
from jax.experimental.pallas import tpu as pltpu

F32 = jnp.float32
BF16 = jnp.bfloat16

D_MODEL = 2048
D_GLA = 1024
GLA_HEADS = 4
GLA_DK = 128
GLA_DV = 256
GLA_KW = GLA_HEADS * GLA_DK
GATE_RANK = 16
GATE_TAU = 16.0
SWA_HEAD_DIM = 64
SWA_Q_HEADS = 16
SWA_KV_HEADS = 2
SWA_GROUP = SWA_Q_HEADS // SWA_KV_HEADS
D_SWA = SWA_Q_HEADS * SWA_HEAD_DIM
SWA_KVW = SWA_KV_HEADS * SWA_HEAD_DIM
WINDOW = 128
D_FF = 4 * D_MODEL
SPLITS = (GLA_KW, GLA_KW, D_GLA, D_GLA, GATE_RANK, D_SWA, SWA_KVW, SWA_KVW)
DEPTH = 1
ALPHA = (2 * DEPTH) ** 0.25
LN_EPS = 1e-5
RMS_EPS = 1e-5

LANES = 128
SUBLANES = 8
MXU_COLS = 256
VMEM_LIMIT_BYTES = 56 * 1024 * 1024

MIX_ROWS = 256
LN_SUB_ROWS = 128
GLA_CHUNK = 128
MLP_ROWS = 512
MLP_FF_TILE = 1024
NEG_BIG = -1e30
LOG2_E = 1.4426950408889634

assert MIX_ROWS % GLA_CHUNK == 0 and MIX_ROWS % WINDOW == 0

def _nn(a, b):
    return lax.dot_general(a, b, (((1,), (0,)), ((), ())), preferred_element_type=F32)


def _nt(a, b):
    return lax.dot_general(a, b, (((1,), (1,)), ((), ())), preferred_element_type=F32)


def _tn(a, b):
    return lax.dot_general(a, b, (((0,), (0,)), ((), ())), preferred_element_type=F32)


def _layer_norm(y, g, b):
    mu = jnp.mean(y, axis=-1, keepdims=True)
    d = y - mu
    var = jnp.mean(d * d, axis=-1, keepdims=True)
    return d * lax.rsqrt(var + LN_EPS) * g + b


def _params(*semantics):
    return pltpu.CompilerParams(dimension_semantics=semantics, vmem_limit_bytes=VMEM_LIMIT_BYTES)


def _resident(shape):
    nd = len(shape)
    return pl.BlockSpec(shape, lambda *_: (0,) * nd, pipeline_mode=pl.Buffered(1))


W_OFFSET = {"qs": 0, "kvs": D_SWA, "qg": D_SWA + 2 * SWA_KVW}
W_OFFSET["kg"] = W_OFFSET["qg"] + GLA_KW
W_OFFSET["vg"] = W_OFFSET["kg"] + GLA_KW
W_OFFSET["gg"] = W_OFFSET["vg"] + D_GLA
W_OFFSET["gl"] = W_OFFSET["gg"] + D_GLA
W_COLS = W_OFFSET["gl"] + LANES
PREP_ROWS = 256


def _prep_kernel(wt_ref, o_ref):
    gla_w = 2 * GLA_KW + 2 * D_GLA
    swa0 = gla_w + GATE_RANK
    swa_w = D_SWA + 2 * SWA_KVW
    o_ref[:, :swa_w] = wt_ref[swa0:swa0 + swa_w, :].T.astype(BF16)
    o_ref[:, swa_w:swa_w + gla_w] = wt_ref[:gla_w, :].T.astype(BF16)
    gate_rank = jnp.concatenate([wt_ref[gla_w:swa0, :], jnp.zeros((LANES - GATE_RANK, wt_ref.shape[1]), F32)], axis=0)
    o_ref[:, swa_w + gla_w:] = gate_rank.T.astype(BF16)


def _prep_w_in(w_in):
    rows, feats = w_in.shape
    return pl.pallas_call(
        _prep_kernel,
        grid=(rows // PREP_ROWS,),
        in_specs=[pl.BlockSpec((feats, PREP_ROWS), lambda i: (0, i))],
        out_specs=pl.BlockSpec((PREP_ROWS, W_COLS), lambda i: (i, 0)),
        out_shape=jax.ShapeDtypeStruct((rows, W_COLS), BF16),
        compiler_params=_params("parallel"),
        name="w_in_prep",
    )(w_in.T)


def _gla_chunk_head(q_ref, k_ref, v_ref, b_ref, gk_ref, gate_ref, st_ref, o_ref, slot, base, h, c, consts):
    eye, small, small_mask, row_cd, sub8, lane8 = consts
    dk, dv = GLA_DK, GLA_DV
    rows = pl.ds(base, c)
    kcols = pl.ds(h * dk, dk)
    vcols = pl.ds(h * dv, dv)

    def brow(r):
        return jnp.broadcast_to(b_ref[slot, pl.ds(r, 1), kcols], (SUBLANES, dk))

    b = b_ref[slot, rows, kcols]
    qb = q_ref[slot, rows, kcols]
    kb = k_ref[slot, rows, kcols]
    q = qb.astype(F32)
    k = kb.astype(F32)
    v = v_ref[slot, rows, vcols]
    b_last = b_ref[slot, pl.ds(base + c - 1, 1), kcols]

    attn = jnp.where(eye, _nt(qb, kb), 0.0)
    for s in small:
        upper = (row_cd & s) != 0
        if s == 1:
            expo = jnp.where(upper, gk_ref[slot, rows, kcols], 0.0)
        else:
            tiles = []
            for r0 in range(0, c, SUBLANES):
                mids = [brow(base + r0 + o + s - 1) for o in range(0, SUBLANES, 2 * s)]
                m8 = mids[-1]
                for idx in range(len(mids) - 2, -1, -1):
                    m8 = jnp.where(sub8 < (idx + 1) * 2 * s, mids[idx], m8)
                tiles.append(m8)
            expo = (b - jnp.concatenate(tiles, axis=0)) * jnp.where(upper, 1.0, -1.0)
        t = (jnp.where(upper, q, k) * jnp.exp2(expo)).astype(BF16)
        attn = jnp.where(small_mask[s], _nt(t, t), attn)

    slabs = [attn[r:r + SUBLANES, :] for r in range(0, c, SUBLANES)]
    s = SUBLANES
    while 2 * s <= c:
        sel, expo = [], []
        for r0 in range(0, c, 2 * s):
            mid = jnp.concatenate([brow(base + r0 + s - 1)] * (s // SUBLANES), axis=0)
            sel += [k[r0:r0 + s, :], q[r0 + s:r0 + 2 * s, :]]
            expo += [mid - b[r0:r0 + s, :], b[r0 + s:r0 + 2 * s, :] - mid]
        t = (jnp.concatenate(sel, axis=0) * jnp.exp2(jnp.concatenate(expo, axis=0))).astype(BF16)
        g_lvl = _nt(t, t)
        for r0 in range(0, c, 2 * s):
            in_lower = (lane8 >= r0) & (lane8 < r0 + s)
            for r in range(r0 + s, r0 + 2 * s, SUBLANES):
                slabs[r // SUBLANES] = jnp.where(in_lower, g_lvl[r:r + SUBLANES, :], slabs[r // SUBLANES])
        s *= 2
    attn = jnp.concatenate(slabs, axis=0).astype(BF16)

    st = st_ref[h]
    q_in = (q * jnp.exp2(b)).astype(BF16)
    o = _nt(q_in, st.astype(BF16)) + _nn(attn, v)
    k_out = (k * jnp.exp2(b_last - b)).astype(BF16)
    st_ref[h] = st * jnp.exp2(b_last) + _tn(v, k_out)

    o = o * lax.rsqrt(jnp.mean(o * o, axis=-1, keepdims=True) + RMS_EPS)
    o_ref[rows, vcols] = (o * gate_ref[slot, rows, vcols].astype(F32)).astype(o_ref.dtype)


def _gla_consts(c):
    row_cc = lax.broadcasted_iota(jnp.int32, (c, c), 0)
    col_cc = lax.broadcasted_iota(jnp.int32, (c, c), 1)
    eye = row_cc == col_cc
    small = [s for s in (1, 2, 4) if 2 * s <= min(c, SUBLANES)]
    small_mask = {s: ((row_cc & s) != 0) & ((col_cc & s) == 0) & ((row_cc ^ col_cc) < 2 * s) for s in small}
    row_cd = lax.broadcasted_iota(jnp.int32, (c, GLA_DK), 0)
    sub8 = lax.broadcasted_iota(jnp.int32, (SUBLANES, GLA_DK), 0)
    lane8 = lax.broadcasted_iota(jnp.int32, (SUBLANES, c), 1)
    return eye, small, small_mask, row_cd, sub8, lane8


def _swa_block(sink_ref, q_ref, slot, row0, k_prev, k_cur, v_prev, v_cur, prev_penalty, o_ref):
    w = WINDOW
    hd = SWA_HEAD_DIM
    rows = pl.ds(row0, w)
    kf = jnp.concatenate([k_prev, k_cur], axis=0).astype(F32)
    vf = jnp.concatenate([v_prev, v_cur], axis=0).astype(F32)
    low_kv = lax.broadcasted_iota(jnp.int32, (2 * w, LANES), 1) < hd
    qi = lax.broadcasted_iota(jnp.int32, (w, w), 0)
    kj = lax.broadcasted_iota(jnp.int32, (w, w), 1)
    from_prev = kj > qi
    low_o = lax.broadcasted_iota(jnp.int32, (w, LANES), 1) < hd
    ones_lo = low_kv.astype(BF16)
    ones_hi = (~low_kv).astype(BF16)
    pairs = SWA_GROUP // 2
    for g in range(SWA_KV_HEADS):
        k_rot = kf if g == 0 else pltpu.roll(kf, hd, axis=1)
        v_rot = vf if g == 0 else pltpu.roll(vf, hd, axis=1)
        k_lo = jnp.where(low_kv, k_rot, 0.0)
        v_lo = jnp.where(low_kv, v_rot, 0.0)
        k_hi = pltpu.roll(k_lo, hd, axis=1).astype(BF16)
        v_hi = pltpu.roll(v_lo, hd, axis=1).astype(BF16)
        vw = (jnp.concatenate([v_lo.astype(BF16), ones_lo], axis=1),
              jnp.concatenate([v_hi, ones_hi], axis=1))
        kk = jnp.concatenate([k_lo.astype(BF16), k_hi], axis=0)
        qst = jnp.concatenate(
            [q_ref[slot, rows, pl.ds((g * pairs + p) * LANES, LANES)] for p in range(pairs)], axis=0)
        s_all = _nt(qst, kk)
        acc = None
        sink_terms = []
        for par in range(2):
            probs, terms = [], []
            for p in range(pairs):
                sink = sink_ref[2 * (g * pairs + p) + par] * LOG2_E
                s2 = s_all[p * w:(p + 1) * w, 2 * w * par:2 * w * (par + 1)]
                s_prev = s2[:, :w] if prev_penalty is None else s2[:, :w] + prev_penalty
                s = jnp.where(from_prev, s_prev, s2[:, w:])
                m = jnp.maximum(jnp.max(s, axis=-1, keepdims=True), sink)
                e = jnp.exp2(s - m)
                probs.append(jnp.concatenate([jnp.where(from_prev, e, 0.0).astype(BF16),
                                              jnp.where(from_prev, 0.0, e).astype(BF16)], axis=1))
                terms.append(jnp.exp2(sink - m))
            part = _nn(jnp.concatenate(probs, axis=0), vw[par])
            acc = part if acc is None else acc + part
            sink_terms.append(terms)
        for p in range(pairs):
            blk = acc[p * w:(p + 1) * w, :]
            denom = blk[:, LANES:] + jnp.where(low_o, sink_terms[0][p], sink_terms[1][p])
            o_ref[rows, pl.ds((g * pairs + p) * LANES, LANES)] = (blk[:, :LANES] / denom).astype(o_ref.dtype)


def _mixer_kernel(sink_ref, x_ref, w_ref, wgk2, bgk, nw, wo_f, wu_f, wd_f,
                  gla_o, swa_o, wo_b, wu_b, wd_b,
                  xb_s, qs_s, kvs_s, qg_s, kg_s, vg_s, gate_s, gk_s, b_s, st_s, kvc_s,
                  *, tiles_per_seq, chunk):
    for src, dst in ((wo_f, wo_b), (wu_f, wu_b), (wd_f, wd_b)):
        dst[...] = src[...].astype(BF16)

    i = pl.program_id(0)
    ws = lax.rem(i, 2)
    rs = 1 - ws
    seq_start = lax.rem(i + tiles_per_seq - 1, tiles_per_seq) == 0
    tm = x_ref.shape[0]
    proj_scratch = (qs_s, kvs_s, qg_s, kg_s, vg_s, gate_s, gk_s, b_s)

    @pl.when(i == 0)
    def _():
        for s in proj_scratch:
            s[1] = jnp.zeros(s.shape[1:], s.dtype)
        kvc_s[...] = jnp.zeros_like(kvc_s)

    @pl.when(jnp.logical_or(i == 0, seq_start))
    def _():
        st_s[...] = jnp.zeros_like(st_s)

    consts = _gla_consts(chunk)
    row = lax.broadcasted_iota(jnp.int32, (chunk, chunk), 0)
    col = lax.broadcasted_iota(jnp.int32, (chunk, chunk), 1)
    tril = (col <= row).astype(BF16)

    def proj(group, c0, width):
        w0 = W_OFFSET[group] + c0
        return _nn(xb_s[...], w_ref[:, w0:w0 + width])

    def plain(group, dst, scale, c0):
        y = proj(group, c0, MXU_COLS)
        dst[ws, :, c0:c0 + MXU_COLS] = (y if scale is None else y * scale).astype(BF16)

    def out_gate(c0):
        g = proj("gg", c0, MXU_COLS)
        gate_s[ws, :, c0:c0 + MXU_COLS] = (nw[:, c0:c0 + MXU_COLS] * g / (1.0 + jnp.exp(-g))).astype(BF16)

    def forget_gate():
        gl = proj("gl", 0, LANES).astype(BF16)
        z = _nn(gl, wgk2[...]) + bgk[...]
        gk = (jnp.minimum(z, 0.0) - jnp.log1p(jnp.exp(-jnp.abs(z)))) * (LOG2_E / GATE_TAU)
        gk_s[ws] = gk
        gk_hi = gk.astype(BF16)
        gk_lo = (gk - gk_hi.astype(F32)).astype(BF16)
        for r0 in range(0, tm, chunk):
            b_s[ws, r0:r0 + chunk, :] = (_nn(tril, gk_hi[r0:r0 + chunk, :]) + _nn(tril, gk_lo[r0:r0 + chunk, :]))

    def swa(j):
        r0 = j * WINDOW
        if j == 0:
            k_prev, v_prev = kvc_s[:, :SWA_KVW], kvc_s[:, SWA_KVW:]
            penalty = jnp.where(seq_start, NEG_BIG, 0.0)
        else:
            k_prev = kvs_s[rs, r0 - WINDOW:r0, :SWA_KVW]
            v_prev = kvs_s[rs, r0 - WINDOW:r0, SWA_KVW:]
            penalty = None
        _swa_block(sink_ref, qs_s, rs, r0, k_prev, kvs_s[rs, r0:r0 + WINDOW, :SWA_KVW],
                   v_prev, kvs_s[rs, r0:r0 + WINDOW, SWA_KVW:], penalty, swa_o)

    def gla(ci):
        for h in range(GLA_HEADS):
            _gla_chunk_head(qg_s, kg_s, vg_s, b_s, gk_s, gate_s, st_s, gla_o, rs, ci * chunk, h, chunk, consts)

    xb_s[...] = x_ref[...].astype(BF16)
    n_swa = tm // WINDOW
    n_gla = tm // chunk
    mix = []
    for n in range(max(n_swa, n_gla)):
        mix += [functools.partial(swa, n)] if n < n_swa else []
        mix += [functools.partial(gla, n)] if n < n_gla else []
    projs = []
    for group, dst, scale in (("qs", qs_s, SWA_HEAD_DIM ** -0.5 * LOG2_E), ("kvs", kvs_s, None),
                              ("qg", qg_s, GLA_DK ** -0.5), ("kg", kg_s, None), ("vg", vg_s, None)):
        projs += [functools.partial(plain, group, dst, scale, c0) for c0 in range(0, dst.shape[2], MXU_COLS)]
    projs += [functools.partial(out_gate, c0) for c0 in range(0, D_GLA, MXU_COLS)] + [forget_gate]
    for n, m in enumerate(mix):
        for p in projs[n * len(projs) // len(mix):(n + 1) * len(projs) // len(mix)]:
            p()
        m()
    kvc_s[...] = kvs_s[rs, tm - WINDOW:tm, :]


def _mixer(x2, w_in, w_gk2, b_gk, norm_w, sinks, seq, w_out, w_up, w_down):
    n = x2.shape[0]
    tm = min(MIX_ROWS, seq)
    chunk = min(GLA_CHUNK, tm)
    n_tiles = n // tm
    wgk2 = jnp.pad(w_gk2.astype(BF16), ((0, LANES - GATE_RANK), (0, 0)))
    bgk = b_gk.reshape(1, GLA_KW).astype(F32)
    nw = jnp.tile(norm_w.astype(F32), GLA_HEADS).reshape(1, D_GLA)
    weights = (_prep_w_in(w_in), wgk2, bgk, nw)
    slot = lambda d, t: pltpu.VMEM((2, tm, d), t)
    out_spec = pl.BlockSpec((tm, D_GLA), lambda i: (jnp.maximum(i - 1, 0), 0))
    blk = lambda i: jnp.minimum(i, n_tiles - 1)
    assert D_MODEL % n_tiles == 0 and D_FF % n_tiles == 0
    cw = D_FF // n_tiles
    per_tile = MLP_FF_TILE // cw
    assert MLP_FF_TILE % cw == 0
    cast_specs = [pl.BlockSpec((D_MODEL // n_tiles, D_MODEL), lambda i: (blk(i), 0)),
                  pl.BlockSpec((D_MODEL, cw), lambda i: (0, blk(i))),
                  pl.BlockSpec((D_FF // n_tiles, D_MODEL), lambda i: (blk(i), 0))]
    cast_out_specs = [cast_specs[0],
                      pl.BlockSpec((None, D_MODEL, cw), lambda i: (blk(i) // per_tile, 0, blk(i) % per_tile)),
                      cast_specs[2]]
    cast_shapes = [jax.ShapeDtypeStruct(w_out.shape, BF16),
                   jax.ShapeDtypeStruct((D_FF // MLP_FF_TILE, D_MODEL, MLP_FF_TILE), BF16),
                   jax.ShapeDtypeStruct(w_down.shape, BF16)]
    return pl.pallas_call(
        functools.partial(_mixer_kernel, tiles_per_seq=seq // tm, chunk=chunk),
        grid=(n_tiles + 1,),
        in_specs=[pl.BlockSpec(memory_space=pltpu.SMEM),
                  pl.BlockSpec((tm, D_MODEL), lambda i: (jnp.minimum(i, n_tiles - 1), 0))]
                 + [_resident(w.shape) for w in weights] + cast_specs,
        out_specs=[out_spec, out_spec] + cast_out_specs,
        out_shape=[jax.ShapeDtypeStruct((n, D_GLA), BF16), jax.ShapeDtypeStruct((n, D_SWA), BF16)] + cast_shapes,
        scratch_shapes=[pltpu.VMEM((tm, D_MODEL), BF16),
                        slot(D_SWA, BF16), slot(2 * SWA_KVW, BF16), slot(GLA_KW, BF16), slot(GLA_KW, BF16),
                        slot(D_GLA, BF16), slot(D_GLA, BF16), slot(GLA_KW, F32), slot(GLA_KW, F32),
                        pltpu.VMEM((GLA_HEADS, GLA_DV, GLA_DK), F32),
                        pltpu.VMEM((WINDOW, 2 * SWA_KVW), BF16)],
        compiler_params=_params("arbitrary"),
        name="token_mixer",
    )(sinks.astype(F32), x2, *weights, w_out, w_up, w_down)


def _channel_kernel(gla_ref, swa_ref, x_ref, wo_ref, g1_ref, b1_ref, wu_ref, wd_ref, g2_ref, b2_ref,
                    o_ref, xb_ref):
    f = pl.program_id(1)
    last = pl.num_programs(1) - 1
    tm = x_ref.shape[0]

    def hidden():
        h = jnp.maximum(_nn(xb_ref[...], wu_ref[...]), 0.0)
        return (h * h).astype(BF16)

    subs = [pl.ds(r0, LN_SUB_ROWS) for r0 in range(0, tm, LN_SUB_ROWS)]

    @pl.when(f == 0)
    def _():
        for n in range(len(subs) + 1):
            if n < len(subs):
                rows = subs[n]
                o_ref[rows, :] = (ALPHA * x_ref[rows, :] + _nn(gla_ref[rows, :], wo_ref[:D_GLA, :])
                                  + _nn(swa_ref[rows, :], wo_ref[D_GLA:, :]))
            if n >= 1:
                rows = subs[n - 1]
                x1 = _layer_norm(o_ref[rows, :], g1_ref[...], b1_ref[...])
                xb_ref[rows, :] = x1.astype(BF16)
                o_ref[rows, :] = ALPHA * x1
        o_ref[...] += _nn(hidden(), wd_ref[...])

    @pl.when(jnp.logical_and(f > 0, f < last))
    def _():
        o_ref[...] += _nn(hidden(), wd_ref[...])

    @pl.when(f == last)
    def _():
        h = hidden()
        for n in range(len(subs) + 1):
            if n < len(subs):
                o_ref[subs[n], :] += _nn(h[n * LN_SUB_ROWS:(n + 1) * LN_SUB_ROWS, :], wd_ref[...])
            if n >= 1:
                rows = subs[n - 1]
                o_ref[rows, :] = _layer_norm(o_ref[rows, :], g2_ref[...], b2_ref[...])


def _channel_mixer(gla, swa, x2, w_out, ln1_g, ln1_b, w_up, w_down, ln2_g, ln2_b):
    n = x2.shape[0]
    tm = min(MLP_ROWS, n)
    tf = MLP_FF_TILE
    assert D_FF // tf >= 2
    row = lambda d: pl.BlockSpec((tm, d), lambda i, f: (i, 0))
    vec = lambda: _resident((1, D_MODEL))
    as_row = lambda t: t.reshape(1, D_MODEL)
    return pl.pallas_call(
        _channel_kernel,
        grid=(n // tm, D_FF // tf),
        in_specs=[row(D_GLA), row(D_SWA), row(D_MODEL), _resident(w_out.shape), vec(), vec(),
                  pl.BlockSpec((None, D_MODEL, tf), lambda i, f: (f, 0, 0)),
                  pl.BlockSpec((tf, D_MODEL), lambda i, f: (f, 0)), vec(), vec()],
        out_specs=row(D_MODEL),
        out_shape=jax.ShapeDtypeStruct((n, D_MODEL), F32),
        scratch_shapes=[pltpu.VMEM((tm, D_MODEL), BF16)],
        compiler_params=_params("parallel", "arbitrary"),
        name="channel_mixer",
    )(gla, swa, x2, w_out, as_row(ln1_g), as_row(ln1_b), w_up, w_down,
      as_row(ln2_g), as_row(ln2_b))


def kernel(x, w_in, w_gk2, b_gk, gla_norm_w, swa_sinks, w_out, ln1_g, ln1_b, w_up, w_down, ln2_g, ln2_b):
    bsz, seq, _ = x.shape
    n = bsz * seq
    x2 = x.reshape(n, D_MODEL)
    for l in range(DEPTH):
        gla, swa, wo, wu, wd = _mixer(x2, w_in[l], w_gk2[l], b_gk[l], gla_norm_w[l], swa_sinks[l], seq,
                                      w_out[l], w_up[l], w_down[l])
        x2 = _channel_mixer(gla, swa, x2, wo, ln1_g[l], ln1_b[l], wu, wd, ln2_g[l], ln2_b[l])
    return x2.reshape(bsz, seq, D_MODEL)
```

```python
import functools

import jax
import jax.numpy as jnp
from jax import lax
from jax.experimental import pallas as pl
from jax.experimental.pallas import tpu as pltpu

F32 = jnp.float32
BF16 = jnp.bfloat16

D_MODEL = 2048
D_GLA = 1024
GLA_HEADS = 4
GLA_DK = 128
GLA_DV = 256
GLA_KW = GLA_HEADS * GLA_DK
GATE_RANK = 16
GATE_TAU = 16.0
SWA_HEAD_DIM = 64
SWA_Q_HEADS = 16
SWA_KV_HEADS = 2
SWA_GROUP = SWA_Q_HEADS // SWA_KV_HEADS
D_SWA = SWA_Q_HEADS * SWA_HEAD_DIM
SWA_KVW = SWA_KV_HEADS * SWA_HEAD_DIM
WINDOW = 128
D_FF = 4 * D_MODEL
SPLITS = (GLA_KW, GLA_KW, D_GLA, D_GLA, GATE_RANK, D_SWA, SWA_KVW, SWA_KVW)
DEPTH = 1
ALPHA = (2 * DEPTH) ** 0.25
LN_EPS = 1e-5
RMS_EPS = 1e-5

LANES = 128
SUBLANES = 8
MXU_COLS = 256
VMEM_LIMIT_BYTES = 56 * 1024 * 1024

MIX_ROWS = 256
PROJ_SUB_ROWS = 128
LN_SUB_ROWS = 128
GLA_CHUNK = 128
MLP_ROWS = 512
MLP_FF_TILE = 1024
NEG_BIG = -1e30
LOG2_E = 1.4426950408889634

assert MIX_ROWS % GLA_CHUNK == 0 and MIX_ROWS % WINDOW == 0

def _nn(a, b):
    return lax.dot_general(a, b, (((1,), (0,)), ((), ())), preferred_element_type=F32)


def _nt(a, b):
    return lax.dot_general(a, b, (((1,), (1,)), ((), ())), preferred_element_type=F32)


def _tn(a, b):
    return lax.dot_general(a, b, (((0,), (0,)), ((), ())), preferred_element_type=F32)


def _layer_norm(y, g, b):
    mu = jnp.mean(y, axis=-1, keepdims=True)
    d = y - mu
    var = jnp.mean(d * d, axis=-1, keepdims=True)
    return d * lax.rsqrt(var + LN_EPS) * g + b


def _params(*semantics):
    return pltpu.CompilerParams(dimension_semantics=semantics, vmem_limit_bytes=VMEM_LIMIT_BYTES)


def _resident(shape):
    nd = len(shape)
    return pl.BlockSpec(shape, lambda *_: (0,) * nd, pipeline_mode=pl.Buffered(1))


W_OFFSET = {"qs": 0, "kvs": D_SWA, "qg": D_SWA + 2 * SWA_KVW}
W_OFFSET["kg"] = W_OFFSET["qg"] + GLA_KW
W_OFFSET["vg"] = W_OFFSET["kg"] + GLA_KW
W_OFFSET["gg"] = W_OFFSET["vg"] + D_GLA
W_OFFSET["gl"] = W_OFFSET["gg"] + D_GLA
W_COLS = W_OFFSET["gl"] + LANES
PREP_ROWS = 256


def _prep_kernel(wt_ref, o_ref):
    gla_w = 2 * GLA_KW + 2 * D_GLA
    swa0 = gla_w + GATE_RANK
    swa_w = D_SWA + 2 * SWA_KVW
    o_ref[:, :swa_w] = wt_ref[swa0:swa0 + swa_w, :].T.astype(BF16)
    o_ref[:, swa_w:swa_w + gla_w] = wt_ref[:gla_w, :].T.astype(BF16)
    gate_rank = jnp.concatenate([wt_ref[gla_w:swa0, :], jnp.zeros((LANES - GATE_RANK, wt_ref.shape[1]), F32)], axis=0)
    o_ref[:, swa_w + gla_w:] = gate_rank.T.astype(BF16)


def _prep_w_in(w_in):
    rows, feats = w_in.shape
    return pl.pallas_call(
        _prep_kernel,
        grid=(rows // PREP_ROWS,),
        in_specs=[pl.BlockSpec((feats, PREP_ROWS), lambda i: (0, i))],
        out_specs=pl.BlockSpec((PREP_ROWS, W_COLS), lambda i: (i, 0)),
        out_shape=jax.ShapeDtypeStruct((rows, W_COLS), BF16),
        compiler_params=_params("parallel"),
        name="w_in_prep",
    )(w_in.T)


def _gla_chunk_head(q_ref, k_ref, v_ref, b_ref, gk_ref, gate_ref, st_ref, o_ref, slot, base, h, c, consts):
    eye, small, small_mask, row_cd, sub8, lane8 = consts
    dk, dv = GLA_DK, GLA_DV
    rows = pl.ds(base, c)
    kcols = pl.ds(h * dk, dk)
    vcols = pl.ds(h * dv, dv)

    def brow(r):
        return jnp.broadcast_to(b_ref[slot, pl.ds(r, 1), kcols], (SUBLANES, dk))

    b = b_ref[slot, rows, kcols]
    qb = q_ref[slot, rows, kcols]
    kb = k_ref[slot, rows, kcols]
    q = qb.astype(F32)
    k = kb.astype(F32)
    v = v_ref[slot, rows, vcols]
    b_last = b_ref[slot, pl.ds(base + c - 1, 1), kcols]

    attn = jnp.where(eye, _nt(qb, kb), 0.0)
    for s in small:
        upper = (row_cd & s) != 0
        if s == 1:
            expo = jnp.where(upper, gk_ref[slot, rows, kcols], 0.0)
        else:
            tiles = []
            for r0 in range(0, c, SUBLANES):
                mids = [brow(base + r0 + o + s - 1) for o in range(0, SUBLANES, 2 * s)]
                m8 = mids[-1]
                for idx in range(len(mids) - 2, -1, -1):
                    m8 = jnp.where(sub8 < (idx + 1) * 2 * s, mids[idx], m8)
                tiles.append(m8)
            expo = (b - jnp.concatenate(tiles, axis=0)) * jnp.where(upper, 1.0, -1.0)
        t = (jnp.where(upper, q, k) * jnp.exp2(expo)).astype(BF16)
        attn = jnp.where(small_mask[s], _nt(t, t), attn)

    slabs = [attn[r:r + SUBLANES, :] for r in range(0, c, SUBLANES)]
    s = SUBLANES
    while 2 * s <= c:
        sel, expo = [], []
        for r0 in range(0, c, 2 * s):
            mid = jnp.concatenate([brow(base + r0 + s - 1)] * (s // SUBLANES), axis=0)
            sel += [k[r0:r0 + s, :], q[r0 + s:r0 + 2 * s, :]]
            expo += [mid - b[r0:r0 + s, :], b[r0 + s:r0 + 2 * s, :] - mid]
        t = (jnp.concatenate(sel, axis=0) * jnp.exp2(jnp.concatenate(expo, axis=0))).astype(BF16)
        g_lvl = _nt(t, t)
        for r0 in range(0, c, 2 * s):
            in_lower = (lane8 >= r0) & (lane8 < r0 + s)
            for r in range(r0 + s, r0 + 2 * s, SUBLANES):
                slabs[r // SUBLANES] = jnp.where(in_lower, g_lvl[r:r + SUBLANES, :], slabs[r // SUBLANES])
        s *= 2
    attn = jnp.concatenate(slabs, axis=0).astype(BF16)

    st = st_ref[h]
    q_in = (q * jnp.exp2(b)).astype(BF16)
    o = _nt(q_in, st.astype(BF16)) + _nn(attn, v)
    k_out = (k * jnp.exp2(b_last - b)).astype(BF16)
    st_ref[h] = st * jnp.exp2(b_last) + _tn(v, k_out)

    o = o * lax.rsqrt(jnp.mean(o * o, axis=-1, keepdims=True) + RMS_EPS)
    o_ref[rows, vcols] = (o * gate_ref[slot, rows, vcols].astype(F32)).astype(o_ref.dtype)


def _gla_consts(c):
    row_cc = lax.broadcasted_iota(jnp.int32, (c, c), 0)
    col_cc = lax.broadcasted_iota(jnp.int32, (c, c), 1)
    eye = row_cc == col_cc
    small = [s for s in (1, 2, 4) if 2 * s <= min(c, SUBLANES)]
    small_mask = {s: ((row_cc & s) != 0) & ((col_cc & s) == 0) & ((row_cc ^ col_cc) < 2 * s) for s in small}
    row_cd = lax.broadcasted_iota(jnp.int32, (c, GLA_DK), 0)
    sub8 = lax.broadcasted_iota(jnp.int32, (SUBLANES, GLA_DK), 0)
    lane8 = lax.broadcasted_iota(jnp.int32, (SUBLANES, c), 1)
    return eye, small, small_mask, row_cd, sub8, lane8


def _swa_block(sink_ref, q_ref, slot, row0, k_prev, k_cur, v_prev, v_cur, prev_penalty, o_ref):
    w = WINDOW
    hd = SWA_HEAD_DIM
    rows = pl.ds(row0, w)
    kf = jnp.concatenate([k_prev, k_cur], axis=0).astype(F32)
    vf = jnp.concatenate([v_prev, v_cur], axis=0).astype(F32)
    low_kv = lax.broadcasted_iota(jnp.int32, (2 * w, LANES), 1) < hd
    qi = lax.broadcasted_iota(jnp.int32, (w, w), 0)
    kj = lax.broadcasted_iota(jnp.int32, (w, w), 1)
    from_prev = kj > qi
    low_o = lax.broadcasted_iota(jnp.int32, (w, LANES), 1) < hd
    ones_lo = low_kv.astype(BF16)
    ones_hi = (~low_kv).astype(BF16)
    pairs = SWA_GROUP // 2
    for g in range(SWA_KV_HEADS):
        k_rot = kf if g == 0 else pltpu.roll(kf, hd, axis=1)
        v_rot = vf if g == 0 else pltpu.roll(vf, hd, axis=1)
        k_lo = jnp.where(low_kv, k_rot, 0.0)
        v_lo = jnp.where(low_kv, v_rot, 0.0)
        k_hi = pltpu.roll(k_lo, hd, axis=1).astype(BF16)
        v_hi = pltpu.roll(v_lo, hd, axis=1).astype(BF16)
        vw = (jnp.concatenate([v_lo.astype(BF16), ones_lo], axis=1),
              jnp.concatenate([v_hi, ones_hi], axis=1))
        kk = jnp.concatenate([k_lo.astype(BF16), k_hi], axis=0)
        qst = jnp.concatenate(
            [q_ref[slot, rows, pl.ds((g * pairs + p) * LANES, LANES)] for p in range(pairs)], axis=0)
        s_all = _nt(qst, kk)
        acc = None
        sink_terms = []
        for par in range(2):
            probs, terms = [], []
            for p in range(pairs):
                sink = sink_ref[2 * (g * pairs + p) + par] * LOG2_E
                s2 = s_all[p * w:(p + 1) * w, 2 * w * par:2 * w * (par + 1)]
                s_prev = s2[:, :w] if prev_penalty is None else s2[:, :w] + prev_penalty
                s = jnp.where(from_prev, s_prev, s2[:, w:])
                m = jnp.maximum(jnp.max(s, axis=-1, keepdims=True), sink)
                e = jnp.exp2(s - m)
                probs.append(jnp.concatenate([jnp.where(from_prev, e, 0.0).astype(BF16),
                                              jnp.where(from_prev, 0.0, e).astype(BF16)], axis=1))
                terms.append(jnp.exp2(sink - m))
            part = _nn(jnp.concatenate(probs, axis=0), vw[par])
            acc = part if acc is None else acc + part
            sink_terms.append(terms)
        for p in range(pairs):
            blk = acc[p * w:(p + 1) * w, :]
            denom = blk[:, LANES:] + jnp.where(low_o, sink_terms[0][p], sink_terms[1][p])
            o_ref[rows, pl.ds((g * pairs + p) * LANES, LANES)] = (blk[:, :LANES] / denom).astype(o_ref.dtype)


def _mixer_kernel(sink_ref, x_ref, w_ref, wgk2, bgk, nw, wo_f, wu_f, wd_f,
                  gla_o, swa_o, wo_b, wu_b, wd_b,
                  xb_s, qs_s, kvs_s, qg_s, kg_s, vg_s, gate_s, gk_s, b_s, st_s, kvc_s,
                  *, tiles_per_seq, chunk):
    for src, dst in ((wo_f, wo_b), (wu_f, wu_b), (wd_f, wd_b)):
        dst[...] = src[...].astype(BF16)

    i = pl.program_id(0)
    ws = lax.rem(i, 2)
    rs = 1 - ws
    seq_start = lax.rem(i + tiles_per_seq - 1, tiles_per_seq) == 0
    tm = x_ref.shape[0]
    proj_scratch = (qs_s, kvs_s, qg_s, kg_s, vg_s, gate_s, gk_s, b_s)

    @pl.when(i == 0)
    def _():
        for s in proj_scratch:
            s[1] = jnp.zeros(s.shape[1:], s.dtype)
        kvc_s[...] = jnp.zeros_like(kvc_s)

    @pl.when(jnp.logical_or(i == 0, seq_start))
    def _():
        st_s[...] = jnp.zeros_like(st_s)

    consts = _gla_consts(chunk)
    row = lax.broadcasted_iota(jnp.int32, (chunk, chunk), 0)
    col = lax.broadcasted_iota(jnp.int32, (chunk, chunk), 1)
    tril = (col <= row).astype(BF16)

    def proj(group, c0, width):
        w0 = W_OFFSET[group] + c0
        return jnp.concatenate([_nn(xb_s[r0:r0 + PROJ_SUB_ROWS, :], w_ref[:, w0:w0 + width])
                                for r0 in range(0, tm, PROJ_SUB_ROWS)], axis=0)

    def plain(group, dst, scale, c0):
        y = proj(group, c0, MXU_COLS)
        dst[ws, :, c0:c0 + MXU_COLS] = (y if scale is None else y * scale).astype(BF16)

    def out_gate(c0):
        g = proj("gg", c0, MXU_COLS)
        gate_s[ws, :, c0:c0 + MXU_COLS] = (nw[:, c0:c0 + MXU_COLS] * g / (1.0 + jnp.exp(-g))).astype(BF16)

    def forget_gate():
        gl = proj("gl", 0, LANES).astype(BF16)
        z = _nn(gl, wgk2[...]) + bgk[...]
        gk = (jnp.minimum(z, 0.0) - jnp.log(1.0 + jnp.exp(-jnp.abs(z)))) * (LOG2_E / GATE_TAU)
        gk_s[ws] = gk
        gk_hi = gk.astype(BF16)
        gk_lo = (gk - gk_hi.astype(F32)).astype(BF16)
        for r0 in range(0, tm, chunk):
            b_s[ws, r0:r0 + chunk, :] = (_nn(tril, gk_hi[r0:r0 + chunk, :]) + _nn(tril, gk_lo[r0:r0 + chunk, :]))

    def swa(j):
        r0 = j * WINDOW
        if j == 0:
            k_prev, v_prev = kvc_s[:, :SWA_KVW], kvc_s[:, SWA_KVW:]
            penalty = jnp.where(seq_start, NEG_BIG, 0.0)
        else:
            k_prev = kvs_s[rs, r0 - WINDOW:r0, :SWA_KVW]
            v_prev = kvs_s[rs, r0 - WINDOW:r0, SWA_KVW:]
            penalty = None
        _swa_block(sink_ref, qs_s, rs, r0, k_prev, kvs_s[rs, r0:r0 + WINDOW, :SWA_KVW],
                   v_prev, kvs_s[rs, r0:r0 + WINDOW, SWA_KVW:], penalty, swa_o)

    def gla(ci):
        for h in range(GLA_HEADS):
            _gla_chunk_head(qg_s, kg_s, vg_s, b_s, gk_s, gate_s, st_s, gla_o, rs, ci * chunk, h, chunk, consts)

    xb_s[...] = x_ref[...].astype(BF16)
    n_swa = tm // WINDOW
    n_gla = tm // chunk
    mix = []
    for n in range(max(n_swa, n_gla)):
        mix += [functools.partial(swa, n)] if n < n_swa else []
        mix += [functools.partial(gla, n)] if n < n_gla else []
    projs = []
    for group, dst, scale in (("qs", qs_s, SWA_HEAD_DIM ** -0.5 * LOG2_E), ("kvs", kvs_s, None),
                              ("qg", qg_s, GLA_DK ** -0.5), ("kg", kg_s, None), ("vg", vg_s, None)):
        projs += [functools.partial(plain, group, dst, scale, c0) for c0 in range(0, dst.shape[2], MXU_COLS)]
    projs += [functools.partial(out_gate, c0) for c0 in range(0, D_GLA, MXU_COLS)] + [forget_gate]
    for n, m in enumerate(mix):
        for p in projs[n * len(projs) // len(mix):(n + 1) * len(projs) // len(mix)]:
            p()
        m()
    kvc_s[...] = kvs_s[rs, tm - WINDOW:tm, :]


def _mixer(x2, w_in, w_gk2, b_gk, norm_w, sinks, seq, w_out, w_up, w_down):
    n = x2.shape[0]
    tm = min(MIX_ROWS, seq)
    chunk = min(GLA_CHUNK, tm)
    n_tiles = n // tm
    wgk2 = jnp.pad(w_gk2.astype(BF16), ((0, LANES - GATE_RANK), (0, 0)))
    bgk = b_gk.reshape(1, GLA_KW).astype(F32)
    nw = jnp.tile(norm_w.astype(F32), GLA_HEADS).reshape(1, D_GLA)
    weights = (_prep_w_in(w_in), wgk2, bgk, nw)
    slot = lambda d, t: pltpu.VMEM((2, tm, d), t)
    out_spec = pl.BlockSpec((tm, D_GLA), lambda i: (jnp.maximum(i - 1, 0), 0))
    blk = lambda i: jnp.minimum(i, n_tiles - 1)
    assert D_MODEL % n_tiles == 0 and D_FF % n_tiles == 0
    cast_specs = [pl.BlockSpec((D_MODEL // n_tiles, D_MODEL), lambda i: (blk(i), 0)),
                  pl.BlockSpec((D_MODEL, D_FF // n_tiles), lambda i: (0, blk(i))),
                  pl.BlockSpec((D_FF // n_tiles, D_MODEL), lambda i: (blk(i), 0))]
    return pl.pallas_call(
        functools.partial(_mixer_kernel, tiles_per_seq=seq // tm, chunk=chunk),
        grid=(n_tiles + 1,),
        in_specs=[pl.BlockSpec(memory_space=pltpu.SMEM),
                  pl.BlockSpec((tm, D_MODEL), lambda i: (jnp.minimum(i, n_tiles - 1), 0))]
                 + [_resident(w.shape) for w in weights] + cast_specs,
        out_specs=[out_spec, out_spec] + cast_specs,
        out_shape=[jax.ShapeDtypeStruct((n, D_GLA), BF16), jax.ShapeDtypeStruct((n, D_SWA), BF16)]
                  + [jax.ShapeDtypeStruct(w.shape, BF16) for w in (w_out, w_up, w_down)],
        scratch_shapes=[pltpu.VMEM((tm, D_MODEL), BF16),
                        slot(D_SWA, BF16), slot(2 * SWA_KVW, BF16), slot(GLA_KW, BF16), slot(GLA_KW, BF16),
                        slot(D_GLA, BF16), slot(D_GLA, BF16), slot(GLA_KW, F32), slot(GLA_KW, F32),
                        pltpu.VMEM((GLA_HEADS, GLA_DV, GLA_DK), F32),
                        pltpu.VMEM((WINDOW, 2 * SWA_KVW), BF16)],
        compiler_params=_params("arbitrary"),
        name="token_mixer",
    )(sinks.astype(F32), x2, *weights, w_out, w_up, w_down)


def _channel_kernel(gla_ref, swa_ref, x_ref, wo_ref, g1_ref, b1_ref, wu_ref, wd_ref, g2_ref, b2_ref,
                    o_ref, xb_ref):
    f = pl.program_id(1)
    last = pl.num_programs(1) - 1
    tm = x_ref.shape[0]

    def hidden():
        h = jnp.maximum(_nn(xb_ref[...], wu_ref[...]), 0.0)
        return (h * h).astype(BF16)

    subs = [pl.ds(r0, LN_SUB_ROWS) for r0 in range(0, tm, LN_SUB_ROWS)]

    @pl.when(f == 0)
    def _():
        for n in range(len(subs) + 1):
            if n < len(subs):
                rows = subs[n]
                o_ref[rows, :] = (ALPHA * x_ref[rows, :] + _nn(gla_ref[rows, :], wo_ref[:D_GLA, :])
                                  + _nn(swa_ref[rows, :], wo_ref[D_GLA:, :]))
            if n >= 1:
                rows = subs[n - 1]
                x1 = _layer_norm(o_ref[rows, :], g1_ref[...], b1_ref[...])
                xb_ref[rows, :] = x1.astype(BF16)
                o_ref[rows, :] = ALPHA * x1
        o_ref[...] += _nn(hidden(), wd_ref[...])

    @pl.when(jnp.logical_and(f > 0, f < last))
    def _():
        o_ref[...] += _nn(hidden(), wd_ref[...])

    @pl.when(f == last)
    def _():
        h = hidden()
        for n in range(len(subs) + 1):
            if n < len(subs):
                o_ref[subs[n], :] += _nn(h[n * LN_SUB_ROWS:(n + 1) * LN_SUB_ROWS, :], wd_ref[...])
            if n >= 1:
                rows = subs[n - 1]
                o_ref[rows, :] = _layer_norm(o_ref[rows, :], g2_ref[...], b2_ref[...])


def _channel_mixer(gla, swa, x2, w_out, ln1_g, ln1_b, w_up, w_down, ln2_g, ln2_b):
    n = x2.shape[0]
    tm = min(MLP_ROWS, n)
    tf = MLP_FF_TILE
    assert D_FF // tf >= 2
    row = lambda d: pl.BlockSpec((tm, d), lambda i, f: (i, 0))
    vec = lambda: _resident((1, D_MODEL))
    as_row = lambda t: t.reshape(1, D_MODEL)
    return pl.pallas_call(
        _channel_kernel,
        grid=(n // tm, D_FF // tf),
        in_specs=[row(D_GLA), row(D_SWA), row(D_MODEL), _resident(w_out.shape), vec(), vec(),
                  pl.BlockSpec((D_MODEL, tf), lambda i, f: (0, f)),
                  pl.BlockSpec((tf, D_MODEL), lambda i, f: (f, 0)), vec(), vec()],
        out_specs=row(D_MODEL),
        out_shape=jax.ShapeDtypeStruct((n, D_MODEL), F32),
        scratch_shapes=[pltpu.VMEM((tm, D_MODEL), BF16)],
        compiler_params=_params("parallel", "arbitrary"),
        name="channel_mixer",
    )(gla, swa, x2, w_out, as_row(ln1_g), as_row(ln1_b), w_up, w_down,
      as_row(ln2_g), as_row(ln2_b))


def kernel(x, w_in, w_gk2, b_gk, gla_norm_w, swa_sinks, w_out, ln1_g, ln1_b, w_up, w_down, ln2_g, ln2_b):
    bsz, seq, _ = x.shape
    n = bsz * seq
    x2 = x.reshape(n, D_MODEL)
    for l in range(DEPTH):
        gla, swa, wo, wu, wd = _mixer(x2, w_in[l], w_gk2[l], b_gk[l], gla_norm_w[l], swa_sinks[l], seq,
                                      w_out[l], w_up[l], w_down[l])
        x2 = _channel_mixer(gla, swa, x2, wo, ln1_g[l], ln1_b[l], wu, wd, ln2_g[l], ln2_b[l])
    return x2.reshape(bsz, seq, D_MODEL)
```

```python
import functools

import jax
import jax.numpy as jnp
from jax import lax
from jax.experimental import pallas as pl
from jax.experimental.pallas import tpu as pltpu

F32 = jnp.float32
BF16 = jnp.bfloat16

D_MODEL = 2048
D_GLA = 1024
GLA_HEADS = 4
GLA_DK = 128
GLA_DV = 256
GLA_KW = GLA_HEADS * GLA_DK
GATE_RANK = 16
GATE_TAU = 16.0
SWA_HEAD_DIM = 64
SWA_Q_HEADS = 16
SWA_KV_HEADS = 2
SWA_GROUP = SWA_Q_HEADS // SWA_KV_HEADS
D_SWA = SWA_Q_HEADS * SWA_HEAD_DIM
SWA_KVW = SWA_KV_HEADS * SWA_HEAD_DIM
WINDOW = 128
D_FF = 4 * D_MODEL
SPLITS = (GLA_KW, GLA_KW, D_GLA, D_GLA, GATE_RANK, D_SWA, SWA_KVW, SWA_KVW)
DEPTH = 1
ALPHA = (2 * DEPTH) ** 0.25
LN_EPS = 1e-5
RMS_EPS = 1e-5

LANES = 128
SUBLANES = 8
MXU_COLS = 256
VMEM_LIMIT_BYTES = 56 * 1024 * 1024

MIX_ROWS = 256
LN_SUB_ROWS = 128
GLA_CHUNK = 128
MLP_ROWS = 512
MLP_FF_TILE = 1024
NEG_BIG = -1e30
LOG2_E = 1.4426950408889634

assert MIX_ROWS % GLA_CHUNK == 0 and MIX_ROWS % WINDOW == 0

def _nn(a, b):
    return lax.dot_general(a, b, (((1,), (0,)), ((), ())), preferred_element_type=F32)


def _nt(a, b):
    return lax.dot_general(a, b, (((1,), (1,)), ((), ())), preferred_element_type=F32)


def _tn(a, b):
    return lax.dot_general(a, b, (((0,), (0,)), ((), ())), preferred_element_type=F32)


def _layer_norm(y, g, b):
    mu = jnp.mean(y, axis=-1, keepdims=True)
    d = y - mu
    var = jnp.mean(d * d, axis=-1, keepdims=True)
    return d * lax.rsqrt(var + LN_EPS) * g + b


def _params(*semantics):
    return pltpu.CompilerParams(dimension_semantics=semantics, vmem_limit_bytes=VMEM_LIMIT_BYTES)


def _resident(shape):
    nd = len(shape)
    return pl.BlockSpec(shape, lambda *_: (0,) * nd, pipeline_mode=pl.Buffered(1))


W_OFFSET = {"qs": 0, "kvs": D_SWA, "qg": D_SWA + 2 * SWA_KVW}
W_OFFSET["kg"] = W_OFFSET["qg"] + GLA_KW
W_OFFSET["vg"] = W_OFFSET["kg"] + GLA_KW
W_OFFSET["gg"] = W_OFFSET["vg"] + D_GLA
W_OFFSET["gl"] = W_OFFSET["gg"] + D_GLA
W_COLS = W_OFFSET["gl"] + LANES
PREP_ROWS = 256


def _prep_kernel(wt_ref, o_ref):
    gla_w = 2 * GLA_KW + 2 * D_GLA
    swa0 = gla_w + GATE_RANK
    swa_w = D_SWA + 2 * SWA_KVW
    o_ref[:, :swa_w] = wt_ref[swa0:swa0 + swa_w, :].T.astype(BF16)
    o_ref[:, swa_w:swa_w + gla_w] = wt_ref[:gla_w, :].T.astype(BF16)
    gate_rank = jnp.concatenate([wt_ref[gla_w:swa0, :], jnp.zeros((LANES - GATE_RANK, wt_ref.shape[1]), F32)], axis=0)
    o_ref[:, swa_w + gla_w:] = gate_rank.T.astype(BF16)


def _prep_w_in(w_in):
    rows, feats = w_in.shape
    return pl.pallas_call(
        _prep_kernel,
        grid=(rows // PREP_ROWS,),
        in_specs=[pl.BlockSpec((feats, PREP_ROWS), lambda i: (0, i))],
        out_specs=pl.BlockSpec((PREP_ROWS, W_COLS), lambda i: (i, 0)),
        out_shape=jax.ShapeDtypeStruct((rows, W_COLS), BF16),
        compiler_params=_params("parallel"),
        name="w_in_prep",
    )(w_in.T)


def _gla_chunk_head(q_ref, k_ref, v_ref, b_ref, gk_ref, gate_ref, st_ref, o_ref, slot, base, h, c, consts):
    eye, small, small_mask, row_cd, sub8, lane8 = consts
    dk, dv = GLA_DK, GLA_DV
    rows = pl.ds(base, c)
    kcols = pl.ds(h * dk, dk)
    vcols = pl.ds(h * dv, dv)

    def brow(r):
        return jnp.broadcast_to(b_ref[slot, pl.ds(r, 1), kcols], (SUBLANES, dk))

    b = b_ref[slot, rows, kcols]
    qb = q_ref[slot, rows, kcols]
    kb = k_ref[slot, rows, kcols]
    q = qb.astype(F32)
    k = kb.astype(F32)
    v = v_ref[slot, rows, vcols]
    b_last = b_ref[slot, pl.ds(base + c - 1, 1), kcols]

    attn = jnp.where(eye, _nt(qb, kb), 0.0)
    for s in small:
        upper = (row_cd & s) != 0
        if s == 1:
            expo = jnp.where(upper, gk_ref[slot, rows, kcols], 0.0)
        else:
            tiles = []
            for r0 in range(0, c, SUBLANES):
                mids = [brow(base + r0 + o + s - 1) for o in range(0, SUBLANES, 2 * s)]
                m8 = mids[-1]
                for idx in range(len(mids) - 2, -1, -1):
                    m8 = jnp.where(sub8 < (idx + 1) * 2 * s, mids[idx], m8)
                tiles.append(m8)
            expo = (b - jnp.concatenate(tiles, axis=0)) * jnp.where(upper, 1.0, -1.0)
        t = (jnp.where(upper, q, k) * jnp.exp2(expo)).astype(BF16)
        attn = jnp.where(small_mask[s], _nt(t, t), attn)

    slabs = [attn[r:r + SUBLANES, :] for r in range(0, c, SUBLANES)]
    s = SUBLANES
    while 2 * s <= c:
        sel, expo = [], []
        for r0 in range(0, c, 2 * s):
            mid = jnp.concatenate([brow(base + r0 + s - 1)] * (s // SUBLANES), axis=0)
            sel += [k[r0:r0 + s, :], q[r0 + s:r0 + 2 * s, :]]
            expo += [mid - b[r0:r0 + s, :], b[r0 + s:r0 + 2 * s, :] - mid]
        t = (jnp.concatenate(sel, axis=0) * jnp.exp2(jnp.concatenate(expo, axis=0))).astype(BF16)
        g_lvl = _nt(t, t)
        for r0 in range(0, c, 2 * s):
            in_lower = (lane8 >= r0) & (lane8 < r0 + s)
            for r in range(r0 + s, r0 + 2 * s, SUBLANES):
                slabs[r // SUBLANES] = jnp.where(in_lower, g_lvl[r:r + SUBLANES, :], slabs[r // SUBLANES])
        s *= 2
    attn = jnp.concatenate(slabs, axis=0).astype(BF16)

    st = st_ref[h]
    q_in = (q * jnp.exp2(b)).astype(BF16)
    o = _nt(q_in, st.astype(BF16)) + _nn(attn, v)
    k_out = (k * jnp.exp2(b_last - b)).astype(BF16)
    st_ref[h] = st * jnp.exp2(b_last) + _tn(v, k_out)

    o = o * lax.rsqrt(jnp.mean(o * o, axis=-1, keepdims=True) + RMS_EPS)
    o_ref[rows, vcols] = (o * gate_ref[slot, rows, vcols].astype(F32)).astype(o_ref.dtype)


def _gla_consts(c):
    row_cc = lax.broadcasted_iota(jnp.int32, (c, c), 0)
    col_cc = lax.broadcasted_iota(jnp.int32, (c, c), 1)
    eye = row_cc == col_cc
    small = [s for s in (1, 2, 4) if 2 * s <= min(c, SUBLANES)]
    small_mask = {s: ((row_cc & s) != 0) & ((col_cc & s) == 0) & ((row_cc ^ col_cc) < 2 * s) for s in small}
    row_cd = lax.broadcasted_iota(jnp.int32, (c, GLA_DK), 0)
    sub8 = lax.broadcasted_iota(jnp.int32, (SUBLANES, GLA_DK), 0)
    lane8 = lax.broadcasted_iota(jnp.int32, (SUBLANES, c), 1)
    return eye, small, small_mask, row_cd, sub8, lane8


def _swa_block(sink_ref, q_ref, slot, row0, k_prev, k_cur, v_prev, v_cur, prev_penalty, o_ref):
    w = WINDOW
    hd = SWA_HEAD_DIM
    rows = pl.ds(row0, w)
    kf = jnp.concatenate([k_prev, k_cur], axis=0).astype(F32)
    vf = jnp.concatenate([v_prev, v_cur], axis=0).astype(F32)
    low_kv = lax.broadcasted_iota(jnp.int32, (2 * w, LANES), 1) < hd
    qi = lax.broadcasted_iota(jnp.int32, (w, w), 0)
    kj = lax.broadcasted_iota(jnp.int32, (w, w), 1)
    from_prev = kj > qi
    low_o = lax.broadcasted_iota(jnp.int32, (w, LANES), 1) < hd
    ones_lo = low_kv.astype(BF16)
    ones_hi = (~low_kv).astype(BF16)
    pairs = SWA_GROUP // 2
    for g in range(SWA_KV_HEADS):
        k_rot = kf if g == 0 else pltpu.roll(kf, hd, axis=1)
        v_rot = vf if g == 0 else pltpu.roll(vf, hd, axis=1)
        k_lo = jnp.where(low_kv, k_rot, 0.0)
        v_lo = jnp.where(low_kv, v_rot, 0.0)
        k_hi = pltpu.roll(k_lo, hd, axis=1).astype(BF16)
        v_hi = pltpu.roll(v_lo, hd, axis=1).astype(BF16)
        vw = (jnp.concatenate([v_lo.astype(BF16), ones_lo], axis=1),
              jnp.concatenate([v_hi, ones_hi], axis=1))
        kk = jnp.concatenate([k_lo.astype(BF16), k_hi], axis=0)
        qst = jnp.concatenate(
            [q_ref[slot, rows, pl.ds((g * pairs + p) * LANES, LANES)] for p in range(pairs)], axis=0)
        s_all = _nt(qst, kk)
        acc = None
        sink_terms = []
        for par in range(2):
            probs, terms = [], []
            for p in range(pairs):
                sink = sink_ref[2 * (g * pairs + p) + par] * LOG2_E
                s2 = s_all[p * w:(p + 1) * w, 2 * w * par:2 * w * (par + 1)]
                s_prev = s2[:, :w] if prev_penalty is None else s2[:, :w] + prev_penalty
                s = jnp.where(from_prev, s_prev, s2[:, w:])
                m = jnp.maximum(jnp.max(s, axis=-1, keepdims=True), sink)
                e = jnp.exp2(s - m)
                probs.append(jnp.concatenate([jnp.where(from_prev, e, 0.0).astype(BF16),
                                              jnp.where(from_prev, 0.0, e).astype(BF16)], axis=1))
                terms.append(jnp.exp2(sink - m))
            part = _nn(jnp.concatenate(probs, axis=0), vw[par])
            acc = part if acc is None else acc + part
            sink_terms.append(terms)
        for p in range(pairs):
            blk = acc[p * w:(p + 1) * w, :]
            denom = blk[:, LANES:] + jnp.where(low_o, sink_terms[0][p], sink_terms[1][p])
            o_ref[rows, pl.ds((g * pairs + p) * LANES, LANES)] = (blk[:, :LANES] / denom).astype(o_ref.dtype)


def _mixer_kernel(sink_ref, x_ref, w_ref, wgk2, bgk, nw, wo_f, wu_f, wd_f,
                  gla_o, swa_o, wo_b, wu_b, wd_b,
                  xb_s, qs_s, kvs_s, qg_s, kg_s, vg_s, gate_s, gk_s, b_s, st_s, kvc_s,
                  *, tiles_per_seq, chunk):
    for src, dst in ((wo_f, wo_b), (wu_f, wu_b), (wd_f, wd_b)):
        dst[...] = src[...].astype(BF16)

    i = pl.program_id(0)
    ws = lax.rem(i, 2)
    rs = 1 - ws
    seq_start = lax.rem(i + tiles_per_seq - 1, tiles_per_seq) == 0
    tm = x_ref.shape[0]
    proj_scratch = (qs_s, kvs_s, qg_s, kg_s, vg_s, gate_s, gk_s, b_s)

    @pl.when(i == 0)
    def _():
        for s in proj_scratch:
            s[1] = jnp.zeros(s.shape[1:], s.dtype)
        kvc_s[...] = jnp.zeros_like(kvc_s)

    @pl.when(jnp.logical_or(i == 0, seq_start))
    def _():
        st_s[...] = jnp.zeros_like(st_s)

    consts = _gla_consts(chunk)
    row = lax.broadcasted_iota(jnp.int32, (chunk, chunk), 0)
    col = lax.broadcasted_iota(jnp.int32, (chunk, chunk), 1)
    tril = (col <= row).astype(BF16)

    def proj(group, c0, width):
        w0 = W_OFFSET[group] + c0
        return _nn(xb_s[...], w_ref[:, w0:w0 + width])

    def plain(group, dst, scale, c0):
        y = proj(group, c0, MXU_COLS)
        dst[ws, :, c0:c0 + MXU_COLS] = (y if scale is None else y * scale).astype(BF16)

    def out_gate(c0):
        g = proj("gg", c0, MXU_COLS)
        gate_s[ws, :, c0:c0 + MXU_COLS] = (nw[:, c0:c0 + MXU_COLS] * g / (1.0 + jnp.exp(-g))).astype(BF16)

    def forget_gate():
        gl = proj("gl", 0, LANES).astype(BF16)
        z = _nn(gl, wgk2[...]) + bgk[...]
        gk = (jnp.minimum(z, 0.0) - jnp.log1p(jnp.exp(-jnp.abs(z)))) * (LOG2_E / GATE_TAU)
        gk_s[ws] = gk
        gk_hi = gk.astype(BF16)
        gk_lo = (gk - gk_hi.astype(F32)).astype(BF16)
        for r0 in range(0, tm, chunk):
            b_s[ws, r0:r0 + chunk, :] = (_nn(tril, gk_hi[r0:r0 + chunk, :]) + _nn(tril, gk_lo[r0:r0 + chunk, :]))

    def swa(j):
        r0 = j * WINDOW
        if j == 0:
            k_prev, v_prev = kvc_s[:, :SWA_KVW], kvc_s[:, SWA_KVW:]
            penalty = jnp.where(seq_start, NEG_BIG, 0.0)
        else:
            k_prev = kvs_s[rs, r0 - WINDOW:r0, :SWA_KVW]
            v_prev = kvs_s[rs, r0 - WINDOW:r0, SWA_KVW:]
            penalty = None
        _swa_block(sink_ref, qs_s, rs, r0, k_prev, kvs_s[rs, r0:r0 + WINDOW, :SWA_KVW],
                   v_prev, kvs_s[rs, r0:r0 + WINDOW, SWA_KVW:], penalty, swa_o)

    def gla(ci):
        for h in range(GLA_HEADS):
            _gla_chunk_head(qg_s, kg_s, vg_s, b_s, gk_s, gate_s, st_s, gla_o, rs, ci * chunk, h, chunk, consts)

    xb_s[...] = x_ref[...].astype(BF16)
    n_swa = tm // WINDOW
    n_gla = tm // chunk
    mix = []
    for n in range(max(n_swa, n_gla)):
        mix += [functools.partial(swa, n)] if n < n_swa else []
        mix += [functools.partial(gla, n)] if n < n_gla else []
    projs = []
    for group, dst, scale in (("qs", qs_s, SWA_HEAD_DIM ** -0.5 * LOG2_E), ("kvs", kvs_s, None),
                              ("qg", qg_s, GLA_DK ** -0.5), ("kg", kg_s, None), ("vg", vg_s, None)):
        projs += [functools.partial(plain, group, dst, scale, c0) for c0 in range(0, dst.shape[2], MXU_COLS)]
    projs += [functools.partial(out_gate, c0) for c0 in range(0, D_GLA, MXU_COLS)] + [forget_gate]
    for n, m in enumerate(mix):
        for p in projs[n * len(projs) // len(mix):(n + 1) * len(projs) // len(mix)]:
            p()
        m()
    kvc_s[...] = kvs_s[rs, tm - WINDOW:tm, :]


def _mixer(x2, w_in, w_gk2, b_gk, norm_w, sinks, seq, w_out, w_up, w_down):
    n = x2.shape[0]
    tm = min(MIX_ROWS, seq)
    chunk = min(GLA_CHUNK, tm)
    n_tiles = n // tm
    wgk2 = jnp.pad(w_gk2.astype(BF16), ((0, LANES - GATE_RANK), (0, 0)))
    bgk = b_gk.reshape(1, GLA_KW).astype(F32)
    nw = jnp.tile(norm_w.astype(F32), GLA_HEADS).reshape(1, D_GLA)
    weights = (_prep_w_in(w_in), wgk2, bgk, nw)
    slot = lambda d, t: pltpu.VMEM((2, tm, d), t)
    out_spec = pl.BlockSpec((tm, D_GLA), lambda i: (jnp.maximum(i - 1, 0), 0))
    blk = lambda i: jnp.minimum(i, n_tiles - 1)
    assert D_MODEL % n_tiles == 0 and D_FF % n_tiles == 0
    cast_specs = [pl.BlockSpec((D_MODEL // n_tiles, D_MODEL), lambda i: (blk(i), 0)),
                  pl.BlockSpec((D_MODEL, D_FF // n_tiles), lambda i: (0, blk(i))),
                  pl.BlockSpec((D_FF // n_tiles, D_MODEL), lambda i: (blk(i), 0))]
    return pl.pallas_call(
        functools.partial(_mixer_kernel, tiles_per_seq=seq // tm, chunk=chunk),
        grid=(n_tiles + 1,),
        in_specs=[pl.BlockSpec(memory_space=pltpu.SMEM),
                  pl.BlockSpec((tm, D_MODEL), lambda i: (jnp.minimum(i, n_tiles - 1), 0))]
                 + [_resident(w.shape) for w in weights] + cast_specs,
        out_specs=[out_spec, out_spec] + cast_specs,
        out_shape=[jax.ShapeDtypeStruct((n, D_GLA), BF16), jax.ShapeDtypeStruct((n, D_SWA), BF16)]
                  + [jax.ShapeDtypeStruct(w.shape, BF16) for w in (w_out, w_up, w_down)],
        scratch_shapes=[pltpu.VMEM((tm, D_MODEL), BF16),
                        slot(D_SWA, BF16), slot(2 * SWA_KVW, BF16), slot(GLA_KW, BF16), slot(GLA_KW, BF16),
                        slot(D_GLA, BF16), slot(D_GLA, BF16), slot(GLA_KW, F32), slot(GLA_KW, F32),
                        pltpu.VMEM((GLA_HEADS, GLA_DV, GLA_DK), F32),
                        pltpu.VMEM((WINDOW, 2 * SWA_KVW), BF16)],
        compiler_params=_params("arbitrary"),
        name="token_mixer",
    )(sinks.astype(F32), x2, *weights, w_out, w_up, w_down)


def _channel_kernel(gla_ref, swa_ref, x_ref, wo_ref, g1_ref, b1_ref, wu_hbm, wd_hbm, g2_ref, b2_ref,
                    o_ref, xb_ref, wu_buf, wd_buf, sem):
    i = pl.program_id(0)
    tm = x_ref.shape[0]
    tf = wu_buf.shape[2]
    nf = wu_hbm.shape[1] // tf
    assert nf % 2 == 0

    def copies(f, slot):
        return (pltpu.make_async_copy(wu_hbm.at[:, pl.ds(f * tf, tf)], wu_buf.at[slot], sem.at[0, slot]),
                pltpu.make_async_copy(wd_hbm.at[pl.ds(f * tf, tf), :], wd_buf.at[slot], sem.at[1, slot]))

    @pl.when(i == 0)
    def _():
        for c in copies(0, 0):
            c.start()

    subs = [pl.ds(r0, LN_SUB_ROWS) for r0 in range(0, tm, LN_SUB_ROWS)]
    for n in range(len(subs) + 1):
        if n < len(subs):
            rows = subs[n]
            o_ref[rows, :] = (ALPHA * x_ref[rows, :] + _nn(gla_ref[rows, :], wo_ref[:D_GLA, :])
                              + _nn(swa_ref[rows, :], wo_ref[D_GLA:, :]))
        if n >= 1:
            rows = subs[n - 1]
            x1 = _layer_norm(o_ref[rows, :], g1_ref[...], b1_ref[...])
            xb_ref[rows, :] = x1.astype(BF16)
            o_ref[rows, :] = ALPHA * x1

    for f in range(nf):
        slot = f % 2
        for c in copies(f, slot):
            c.wait()
        if f + 1 < nf:
            for c in copies(f + 1, 1 - slot):
                c.start()
        h = jnp.maximum(_nn(xb_ref[...], wu_buf[slot]), 0.0)
        h = (h * h).astype(BF16)
        if f + 1 < nf:
            o_ref[...] += _nn(h, wd_buf[slot])
        else:
            for n in range(len(subs) + 1):
                if n < len(subs):
                    o_ref[subs[n], :] += _nn(h[n * LN_SUB_ROWS:(n + 1) * LN_SUB_ROWS, :], wd_buf[slot])
                if n >= 1:
                    rows = subs[n - 1]
                    o_ref[rows, :] = _layer_norm(o_ref[rows, :], g2_ref[...], b2_ref[...])

    @pl.when(i + 1 < pl.num_programs(0))
    def _():
        for c in copies(0, 0):
            c.start()


def _channel_mixer(gla, swa, x2, w_out, ln1_g, ln1_b, w_up, w_down, ln2_g, ln2_b):
    n = x2.shape[0]
    tm = min(MLP_ROWS, n)
    tf = MLP_FF_TILE
    row = lambda d: pl.BlockSpec((tm, d), lambda i: (i, 0))
    vec = lambda: _resident((1, D_MODEL))
    as_row = lambda t: t.reshape(1, D_MODEL)
    hbm = lambda: pl.BlockSpec(memory_space=pl.ANY)
    return pl.pallas_call(
        _channel_kernel,
        grid=(n // tm,),
        in_specs=[row(D_GLA), row(D_SWA), row(D_MODEL), _resident(w_out.shape), vec(), vec(),
                  hbm(), hbm(), vec(), vec()],
        out_specs=row(D_MODEL),
        out_shape=jax.ShapeDtypeStruct((n, D_MODEL), F32),
        scratch_shapes=[pltpu.VMEM((tm, D_MODEL), BF16),
                        pltpu.VMEM((2, D_MODEL, tf), BF16), pltpu.VMEM((2, tf, D_MODEL), BF16),
                        pltpu.SemaphoreType.DMA((2, 2))],
        compiler_params=_params("arbitrary"),
        name="channel_mixer",
    )(gla, swa, x2, w_out, as_row(ln1_g), as_row(ln1_b), w_up, w_down,
      as_row(ln2_g), as_row(ln2_b))


def kernel(x, w_in, w_gk2, b_gk, gla_norm_w, swa_sinks, w_out, ln1_g, ln1_b, w_up, w_down, ln2_g, ln2_b):
    bsz, seq, _ = x.shape
    n = bsz * seq
    x2 = x.reshape(n, D_MODEL)
    for l in range(DEPTH):
        gla, swa, wo, wu, wd = _mixer(x2, w_in[l], w_gk2[l], b_gk[l], gla_norm_w[l], swa_sinks[l], seq,
                                      w_out[l], w_up[l], w_down[l])
        x2 = _channel_mixer(gla, swa, x2, wo, ln1_g[l], ln1_b[l], wu, wd, ln2_g[l], ln2_b[l])
    return x2.reshape(bsz, seq, D_MODEL)
```

```python
import functools

import jax
import jax.numpy as jnp
from jax import lax
from jax.experimental import pallas as pl
from jax.experimental.pallas import tpu as pltpu

F32 = jnp.float32
BF16 = jnp.bfloat16

D_MODEL = 2048
D_GLA = 1024
GLA_HEADS = 4
GLA_DK = 128
GLA_DV = 256
GLA_KW = GLA_HEADS * GLA_DK
GATE_RANK = 16
GATE_TAU = 16.0
SWA_HEAD_DIM = 64
SWA_Q_HEADS = 16
SWA_KV_HEADS = 2
SWA_GROUP = SWA_Q_HEADS // SWA_KV_HEADS
D_SWA = SWA_Q_HEADS * SWA_HEAD_DIM
SWA_KVW = SWA_KV_HEADS * SWA_HEAD_DIM
WINDOW = 128
D_FF = 4 * D_MODEL
SPLITS = (GLA_KW, GLA_KW, D_GLA, D_GLA, GATE_RANK, D_SWA, SWA_KVW, SWA_KVW)
DEPTH = 1
ALPHA = (2 * DEPTH) ** 0.25
LN_EPS = 1e-5
RMS_EPS = 1e-5

LANES = 128
SUBLANES = 8
MXU_COLS = 256
VMEM_LIMIT_BYTES = 56 * 1024 * 1024

MIX_ROWS = 256
LN_SUB_ROWS = 128
GLA_CHUNK = 128
MLP_ROWS = 512
MLP_FF_TILE = 1024
NEG_BIG = -1e30
LOG2_E = 1.4426950408889634

assert MIX_ROWS % GLA_CHUNK == 0 and MIX_ROWS % WINDOW == 0

def _nn(a, b):
    return lax.dot_general(a, b, (((1,), (0,)), ((), ())), preferred_element_type=F32)


def _nt(a, b):
    return lax.dot_general(a, b, (((1,), (1,)), ((), ())), preferred_element_type=F32)


def _tn(a, b):
    return lax.dot_general(a, b, (((0,), (0,)), ((), ())), preferred_element_type=F32)


def _layer_norm(y, g, b):
    mu = jnp.mean(y, axis=-1, keepdims=True)
    d = y - mu
    var = jnp.mean(d * d, axis=-1, keepdims=True)
    return d * lax.rsqrt(var + LN_EPS) * g + b


def _params(*semantics):
    return pltpu.CompilerParams(dimension_semantics=semantics, vmem_limit_bytes=VMEM_LIMIT_BYTES)


def _resident(shape):
    nd = len(shape)
    return pl.BlockSpec(shape, lambda *_: (0,) * nd, pipeline_mode=pl.Buffered(1))


W_OFFSET = {"qs": 0, "kvs": D_SWA, "qg": D_SWA + 2 * SWA_KVW}
W_OFFSET["kg"] = W_OFFSET["qg"] + GLA_KW
W_OFFSET["vg"] = W_OFFSET["kg"] + GLA_KW
W_OFFSET["gg"] = W_OFFSET["vg"] + D_GLA
W_OFFSET["gl"] = W_OFFSET["gg"] + D_GLA
W_COLS = W_OFFSET["gl"] + LANES
PREP_ROWS = 256


def _prep_kernel(wt_ref, o_ref):
    gla_w = 2 * GLA_KW + 2 * D_GLA
    swa0 = gla_w + GATE_RANK
    swa_w = D_SWA + 2 * SWA_KVW
    o_ref[:, :swa_w] = wt_ref[swa0:swa0 + swa_w, :].T.astype(BF16)
    o_ref[:, swa_w:swa_w + gla_w] = wt_ref[:gla_w, :].T.astype(BF16)
    gate_rank = jnp.concatenate([wt_ref[gla_w:swa0, :], jnp.zeros((LANES - GATE_RANK, wt_ref.shape[1]), F32)], axis=0)
    o_ref[:, swa_w + gla_w:] = gate_rank.T.astype(BF16)


def _prep_w_in(w_in):
    rows, feats = w_in.shape
    return pl.pallas_call(
        _prep_kernel,
        grid=(rows // PREP_ROWS,),
        in_specs=[pl.BlockSpec((feats, PREP_ROWS), lambda i: (0, i))],
        out_specs=pl.BlockSpec((PREP_ROWS, W_COLS), lambda i: (i, 0)),
        out_shape=jax.ShapeDtypeStruct((rows, W_COLS), BF16),
        compiler_params=_params("parallel"),
        name="w_in_prep",
    )(w_in.T)


def _gla_chunk_head(q_ref, k_ref, v_ref, b_ref, gk_ref, gate_ref, st_ref, o_ref, slot, base, h, c, consts):
    eye, small, small_mask, row_cd, sub8, lane8 = consts
    dk, dv = GLA_DK, GLA_DV
    rows = pl.ds(base, c)
    kcols = pl.ds(h * dk, dk)
    vcols = pl.ds(h * dv, dv)

    def brow(r):
        return jnp.broadcast_to(b_ref[slot, pl.ds(r, 1), kcols], (SUBLANES, dk))

    b = b_ref[slot, rows, kcols]
    qb = q_ref[slot, rows, kcols]
    kb = k_ref[slot, rows, kcols]
    q = qb.astype(F32)
    k = kb.astype(F32)
    v = v_ref[slot, rows, vcols]
    b_last = b_ref[slot, pl.ds(base + c - 1, 1), kcols]

    attn = jnp.where(eye, _nt(qb, kb), 0.0)
    for s in small:
        upper = (row_cd & s) != 0
        if s == 1:
            expo = jnp.where(upper, gk_ref[slot, rows, kcols], 0.0)
        else:
            tiles = []
            for r0 in range(0, c, SUBLANES):
                mids = [brow(base + r0 + o + s - 1) for o in range(0, SUBLANES, 2 * s)]
                m8 = mids[-1]
                for idx in range(len(mids) - 2, -1, -1):
                    m8 = jnp.where(sub8 < (idx + 1) * 2 * s, mids[idx], m8)
                tiles.append(m8)
            expo = (b - jnp.concatenate(tiles, axis=0)) * jnp.where(upper, 1.0, -1.0)
        t = (jnp.where(upper, q, k) * jnp.exp2(expo)).astype(BF16)
        attn = jnp.where(small_mask[s], _nt(t, t), attn)

    slabs = [attn[r:r + SUBLANES, :] for r in range(0, c, SUBLANES)]
    s = SUBLANES
    while 2 * s <= c:
        sel, expo = [], []
        for r0 in range(0, c, 2 * s):
            mid = jnp.concatenate([brow(base + r0 + s - 1)] * (s // SUBLANES), axis=0)
            sel += [k[r0:r0 + s, :], q[r0 + s:r0 + 2 * s, :]]
            expo += [mid - b[r0:r0 + s, :], b[r0 + s:r0 + 2 * s, :] - mid]
        t = (jnp.concatenate(sel, axis=0) * jnp.exp2(jnp.concatenate(expo, axis=0))).astype(BF16)
        g_lvl = _nt(t, t)
        for r0 in range(0, c, 2 * s):
            in_lower = (lane8 >= r0) & (lane8 < r0 + s)
            for r in range(r0 + s, r0 + 2 * s, SUBLANES):
                slabs[r // SUBLANES] = jnp.where(in_lower, g_lvl[r:r + SUBLANES, :], slabs[r // SUBLANES])
        s *= 2
    attn = jnp.concatenate(slabs, axis=0).astype(BF16)

    st = st_ref[h]
    q_in = (q * jnp.exp2(b)).astype(BF16)
    o = _nt(q_in, st.astype(BF16)) + _nn(attn, v)
    k_out = (k * jnp.exp2(b_last - b)).astype(BF16)
    st_ref[h] = st * jnp.exp2(b_last) + _tn(v, k_out)

    o = o * lax.rsqrt(jnp.mean(o * o, axis=-1, keepdims=True) + RMS_EPS)
    o_ref[rows, vcols] = (o * gate_ref[slot, rows, vcols].astype(F32)).astype(o_ref.dtype)


def _gla_consts(c):
    row_cc = lax.broadcasted_iota(jnp.int32, (c, c), 0)
    col_cc = lax.broadcasted_iota(jnp.int32, (c, c), 1)
    eye = row_cc == col_cc
    small = [s for s in (1, 2, 4) if 2 * s <= min(c, SUBLANES)]
    small_mask = {s: ((row_cc & s) != 0) & ((col_cc & s) == 0) & ((row_cc ^ col_cc) < 2 * s) for s in small}
    row_cd = lax.broadcasted_iota(jnp.int32, (c, GLA_DK), 0)
    sub8 = lax.broadcasted_iota(jnp.int32, (SUBLANES, GLA_DK), 0)
    lane8 = lax.broadcasted_iota(jnp.int32, (SUBLANES, c), 1)
    return eye, small, small_mask, row_cd, sub8, lane8


def _swa_block(sink_ref, q_ref, slot, row0, k_prev, k_cur, v_prev, v_cur, prev_penalty, o_ref):
    w = WINDOW
    hd = SWA_HEAD_DIM
    rows = pl.ds(row0, w)
    kf = jnp.concatenate([k_prev, k_cur], axis=0).astype(F32)
    vf = jnp.concatenate([v_prev, v_cur], axis=0).astype(F32)
    low_kv = lax.broadcasted_iota(jnp.int32, (2 * w, LANES), 1) < hd
    key_j = lax.broadcasted_iota(jnp.int32, (w, w), 0)
    qry_i = lax.broadcasted_iota(jnp.int32, (w, w), 1)
    from_prev = key_j > qry_i
    pairs = SWA_GROUP // 2
    for g in range(SWA_KV_HEADS):
        k_rot = kf if g == 0 else pltpu.roll(kf, hd, axis=1)
        v_rot = vf if g == 0 else pltpu.roll(vf, hd, axis=1)
        k_lo = jnp.where(low_kv, k_rot, 0.0)
        v_lo = jnp.where(low_kv, v_rot, 0.0).astype(BF16)
        k_hi = pltpu.roll(k_lo, hd, axis=1).astype(BF16)
        kk = jnp.concatenate([k_lo.astype(BF16), k_hi], axis=0)
        qst = jnp.concatenate(
            [q_ref[slot, rows, pl.ds((g * pairs + p) * LANES, LANES)] for p in range(pairs)], axis=0)
        s_all = _nt(kk, qst)
        probs, inv = [], []
        for par in range(2):
            for p in range(pairs):
                sink = sink_ref[2 * (g * pairs + p) + par] * LOG2_E
                s2 = s_all[2 * w * par:2 * w * (par + 1), p * w:(p + 1) * w]
                s_prev = s2[:w, :] if prev_penalty is None else s2[:w, :] + prev_penalty
                s = jnp.where(from_prev, s_prev, s2[w:, :])
                m = jnp.maximum(jnp.max(s, axis=0, keepdims=True), sink)
                e = jnp.exp2(s - m)
                probs.append(jnp.concatenate([jnp.where(from_prev, e, 0.0).astype(BF16),
                                              jnp.where(from_prev, 0.0, e).astype(BF16)], axis=0))
                inv.append(1.0 / (jnp.sum(e, axis=0, keepdims=True) + jnp.exp2(sink - m)))
        out_t = _tn(v_lo, jnp.concatenate(probs, axis=1))
        for p in range(pairs):
            halves = [out_t[:hd, (par * pairs + p) * w:(par * pairs + p + 1) * w] * inv[par * pairs + p]
                      for par in range(2)]
            o_ref[rows, pl.ds((g * pairs + p) * LANES, LANES)] = jnp.concatenate(halves, axis=0).T.astype(o_ref.dtype)


def _mixer_kernel(sink_ref, x_ref, w_ref, wgk2, bgk, nw, wo_f, wu_f, wd_f,
                  gla_o, swa_o, wo_b, wu_b, wd_b,
                  xb_s, qs_s, kvs_s, qg_s, kg_s, vg_s, gate_s, gk_s, b_s, st_s, kvc_s,
                  *, tiles_per_seq, chunk):
    for src, dst in ((wo_f, wo_b), (wu_f, wu_b), (wd_f, wd_b)):
        dst[...] = src[...].astype(BF16)

    i = pl.program_id(0)
    ws = lax.rem(i, 2)
    rs = 1 - ws
    seq_start = lax.rem(i + tiles_per_seq - 1, tiles_per_seq) == 0
    tm = x_ref.shape[0]
    proj_scratch = (qs_s, kvs_s, qg_s, kg_s, vg_s, gate_s, gk_s, b_s)

    @pl.when(i == 0)
    def _():
        for s in proj_scratch:
            s[1] = jnp.zeros(s.shape[1:], s.dtype)
        kvc_s[...] = jnp.zeros_like(kvc_s)

    @pl.when(jnp.logical_or(i == 0, seq_start))
    def _():
        st_s[...] = jnp.zeros_like(st_s)

    consts = _gla_consts(chunk)
    row = lax.broadcasted_iota(jnp.int32, (chunk, chunk), 0)
    col = lax.broadcasted_iota(jnp.int32, (chunk, chunk), 1)
    tril = (col <= row).astype(BF16)

    def proj(group, c0, width):
        w0 = W_OFFSET[group] + c0
        return _nn(xb_s[...], w_ref[:, w0:w0 + width])

    def plain(group, dst, scale, c0):
        y = proj(group, c0, MXU_COLS)
        dst[ws, :, c0:c0 + MXU_COLS] = (y if scale is None else y * scale).astype(BF16)

    def out_gate(c0):
        g = proj("gg", c0, MXU_COLS)
        gate_s[ws, :, c0:c0 + MXU_COLS] = (nw[:, c0:c0 + MXU_COLS] * g / (1.0 + jnp.exp(-g))).astype(BF16)

    def forget_gate():
        gl = proj("gl", 0, LANES).astype(BF16)
        z = _nn(gl, wgk2[...]) + bgk[...]
        gk = (jnp.minimum(z, 0.0) - jnp.log1p(jnp.exp(-jnp.abs(z)))) * (LOG2_E / GATE_TAU)
        gk_s[ws] = gk
        gk_hi = gk.astype(BF16)
        gk_lo = (gk - gk_hi.astype(F32)).astype(BF16)
        for r0 in range(0, tm, chunk):
            b_s[ws, r0:r0 + chunk, :] = (_nn(tril, gk_hi[r0:r0 + chunk, :]) + _nn(tril, gk_lo[r0:r0 + chunk, :]))

    def swa(j):
        r0 = j * WINDOW
        if j == 0:
            k_prev, v_prev = kvc_s[:, :SWA_KVW], kvc_s[:, SWA_KVW:]
            penalty = jnp.where(seq_start, NEG_BIG, 0.0)
        else:
            k_prev = kvs_s[rs, r0 - WINDOW:r0, :SWA_KVW]
            v_prev = kvs_s[rs, r0 - WINDOW:r0, SWA_KVW:]
            penalty = None
        _swa_block(sink_ref, qs_s, rs, r0, k_prev, kvs_s[rs, r0:r0 + WINDOW, :SWA_KVW],
                   v_prev, kvs_s[rs, r0:r0 + WINDOW, SWA_KVW:], penalty, swa_o)

    def gla(ci):
        for h in range(GLA_HEADS):
            _gla_chunk_head(qg_s, kg_s, vg_s, b_s, gk_s, gate_s, st_s, gla_o, rs, ci * chunk, h, chunk, consts)

    xb_s[...] = x_ref[...].astype(BF16)
    n_swa = tm // WINDOW
    n_gla = tm // chunk
    mix = []
    for n in range(max(n_swa, n_gla)):
        mix += [functools.partial(swa, n)] if n < n_swa else []
        mix += [functools.partial(gla, n)] if n < n_gla else []
    projs = []
    for group, dst, scale in (("qs", qs_s, SWA_HEAD_DIM ** -0.5 * LOG2_E), ("kvs", kvs_s, None),
                              ("qg", qg_s, GLA_DK ** -0.5), ("kg", kg_s, None), ("vg", vg_s, None)):
        projs += [functools.partial(plain, group, dst, scale, c0) for c0 in range(0, dst.shape[2], MXU_COLS)]
    projs += [functools.partial(out_gate, c0) for c0 in range(0, D_GLA, MXU_COLS)] + [forget_gate]
    for n, m in enumerate(mix):
        for p in projs[n * len(projs) // len(mix):(n + 1) * len(projs) // len(mix)]:
            p()
        m()
    kvc_s[...] = kvs_s[rs, tm - WINDOW:tm, :]


def _mixer(x2, w_in, w_gk2, b_gk, norm_w, sinks, seq, w_out, w_up, w_down):
    n = x2.shape[0]
    tm = min(MIX_ROWS, seq)
    chunk = min(GLA_CHUNK, tm)
    n_tiles = n // tm
    wgk2 = jnp.pad(w_gk2.astype(BF16), ((0, LANES - GATE_RANK), (0, 0)))
    bgk = b_gk.reshape(1, GLA_KW).astype(F32)
    nw = jnp.tile(norm_w.astype(F32), GLA_HEADS).reshape(1, D_GLA)
    weights = (_prep_w_in(w_in), wgk2, bgk, nw)
    slot = lambda d, t: pltpu.VMEM((2, tm, d), t)
    out_spec = pl.BlockSpec((tm, D_GLA), lambda i: (jnp.maximum(i - 1, 0), 0))
    blk = lambda i: jnp.minimum(i, n_tiles - 1)
    assert D_MODEL % n_tiles == 0 and D_FF % n_tiles == 0
    cast_specs = [pl.BlockSpec((D_MODEL // n_tiles, D_MODEL), lambda i: (blk(i), 0)),
                  pl.BlockSpec((D_MODEL, D_FF // n_tiles), lambda i: (0, blk(i))),
                  pl.BlockSpec((D_FF // n_tiles, D_MODEL), lambda i: (blk(i), 0))]
    return pl.pallas_call(
        functools.partial(_mixer_kernel, tiles_per_seq=seq // tm, chunk=chunk),
        grid=(n_tiles + 1,),
        in_specs=[pl.BlockSpec(memory_space=pltpu.SMEM),
                  pl.BlockSpec((tm, D_MODEL), lambda i: (jnp.minimum(i, n_tiles - 1), 0))]
                 + [_resident(w.shape) for w in weights] + cast_specs,
        out_specs=[out_spec, out_spec] + cast_specs,
        out_shape=[jax.ShapeDtypeStruct((n, D_GLA), BF16), jax.ShapeDtypeStruct((n, D_SWA), BF16)]
                  + [jax.ShapeDtypeStruct(w.shape, BF16) for w in (w_out, w_up, w_down)],
        scratch_shapes=[pltpu.VMEM((tm, D_MODEL), BF16),
                        slot(D_SWA, BF16), slot(2 * SWA_KVW, BF16), slot(GLA_KW, BF16), slot(GLA_KW, BF16),
                        slot(D_GLA, BF16), slot(D_GLA, BF16), slot(GLA_KW, F32), slot(GLA_KW, F32),
                        pltpu.VMEM((GLA_HEADS, GLA_DV, GLA_DK), F32),
                        pltpu.VMEM((WINDOW, 2 * SWA_KVW), BF16)],
        compiler_params=_params("arbitrary"),
        name="token_mixer",
    )(sinks.astype(F32), x2, *weights, w_out, w_up, w_down)


def _channel_kernel(gla_ref, swa_ref, x_ref, wo_ref, g1_ref, b1_ref, wu_ref, wd_ref, g2_ref, b2_ref,
                    o_ref, xb_ref):
    f = pl.program_id(1)
    last = pl.num_programs(1) - 1
    tm = x_ref.shape[0]

    def hidden():
        h = jnp.maximum(_nn(xb_ref[...], wu_ref[...]), 0.0)
        return (h * h).astype(BF16)

    subs = [pl.ds(r0, LN_SUB_ROWS) for r0 in range(0, tm, LN_SUB_ROWS)]

    @pl.when(f == 0)
    def _():
        for n in range(len(subs) + 1):
            if n < len(subs):
                rows = subs[n]
                o_ref[rows, :] = (ALPHA * x_ref[rows, :] + _nn(gla_ref[rows, :], wo_ref[:D_GLA, :])
                                  + _nn(swa_ref[rows, :], wo_ref[D_GLA:, :]))
            if n >= 1:
                rows = subs[n - 1]
                x1 = _layer_norm(o_ref[rows, :], g1_ref[...], b1_ref[...])
                xb_ref[rows, :] = x1.astype(BF16)
                o_ref[rows, :] = ALPHA * x1
        o_ref[...] += _nn(hidden(), wd_ref[...])

    @pl.when(jnp.logical_and(f > 0, f < last))
    def _():
        o_ref[...] += _nn(hidden(), wd_ref[...])

    @pl.when(f == last)
    def _():
        h = hidden()
        for n in range(len(subs) + 1):
            if n < len(subs):
                o_ref[subs[n], :] += _nn(h[n * LN_SUB_ROWS:(n + 1) * LN_SUB_ROWS, :], wd_ref[...])
            if n >= 1:
                rows = subs[n - 1]
                o_ref[rows, :] = _layer_norm(o_ref[rows, :], g2_ref[...], b2_ref[...])


def _channel_mixer(gla, swa, x2, w_out, ln1_g, ln1_b, w_up, w_down, ln2_g, ln2_b):
    n = x2.shape[0]
    tm = min(MLP_ROWS, n)
    tf = MLP_FF_TILE
    assert D_FF // tf >= 2
    row = lambda d: pl.BlockSpec((tm, d), lambda i, f: (i, 0))
    vec = lambda: _resident((1, D_MODEL))
    as_row = lambda t: t.reshape(1, D_MODEL)
    return pl.pallas_call(
        _channel_kernel,
        grid=(n // tm, D_FF // tf),
        in_specs=[row(D_GLA), row(D_SWA), row(D_MODEL), _resident(w_out.shape), vec(), vec(),
                  pl.BlockSpec((D_MODEL, tf), lambda i, f: (0, f)),
                  pl.BlockSpec((tf, D_MODEL), lambda i, f: (f, 0)), vec(), vec()],
        out_specs=row(D_MODEL),
        out_shape=jax.ShapeDtypeStruct((n, D_MODEL), F32),
        scratch_shapes=[pltpu.VMEM((tm, D_MODEL), BF16)],
        compiler_params=_params("parallel", "arbitrary"),
        name="channel_mixer",
    )(gla, swa, x2, w_out, as_row(ln1_g), as_row(ln1_b), w_up, w_down,
      as_row(ln2_g), as_row(ln2_b))


def kernel(x, w_in, w_gk2, b_gk, gla_norm_w, swa_sinks, w_out, ln1_g, ln1_b, w_up, w_down, ln2_g, ln2_b):
    bsz, seq, _ = x.shape
    n = bsz * seq
    x2 = x.reshape(n, D_MODEL)
    for l in range(DEPTH):
        gla, swa, wo, wu, wd = _mixer(x2, w_in[l], w_gk2[l], b_gk[l], gla_norm_w[l], swa_sinks[l], seq,
                                      w_out[l], w_up[l], w_down[l])
        x2 = _channel_mixer(gla, swa, x2, wo, ln1_g[l], ln1_b[l], wu, wd, ln2_g[l], ln2_b[l])
    return x2.reshape(bsz, seq, D_MODEL)
```

```python
import functools

import jax
import jax.numpy as jnp
from jax import lax
from jax.experimental import pallas as pl
from jax.experimental.pallas import tpu as pltpu

F32 = jnp.float32
BF16 = jnp.bfloat16

D_MODEL = 2048
D_GLA = 1024
GLA_HEADS = 4
GLA_DK = 128
GLA_DV = 256
GLA_KW = GLA_HEADS * GLA_DK
GATE_RANK = 16
GATE_TAU = 16.0
SWA_HEAD_DIM = 64
SWA_Q_HEADS = 16
SWA_KV_HEADS = 2
SWA_GROUP = SWA_Q_HEADS // SWA_KV_HEADS
D_SWA = SWA_Q_HEADS * SWA_HEAD_DIM
SWA_KVW = SWA_KV_HEADS * SWA_HEAD_DIM
WINDOW = 128
D_FF = 4 * D_MODEL
SPLITS = (GLA_KW, GLA_KW, D_GLA, D_GLA, GATE_RANK, D_SWA, SWA_KVW, SWA_KVW)
DEPTH = 1
ALPHA = (2 * DEPTH) ** 0.25
LN_EPS = 1e-5
RMS_EPS = 1e-5

LANES = 128
SUBLANES = 8
MXU_COLS = 256
VMEM_LIMIT_BYTES = 56 * 1024 * 1024

MIX_ROWS = 256
LN_SUB_ROWS = 128
GLA_CHUNK = 128
MLP_ROWS = 512
MLP_FF_TILE = 1024
NEG_BIG = -1e30
LOG2_E = 1.4426950408889634

assert MIX_ROWS % GLA_CHUNK == 0 and MIX_ROWS % WINDOW == 0

def _nn(a, b):
    return lax.dot_general(a, b, (((1,), (0,)), ((), ())), preferred_element_type=F32)


def _nt(a, b):
    return lax.dot_general(a, b, (((1,), (1,)), ((), ())), preferred_element_type=F32)


def _tn(a, b):
    return lax.dot_general(a, b, (((0,), (0,)), ((), ())), preferred_element_type=F32)


def _layer_norm(y, g, b):
    mu = jnp.mean(y, axis=-1, keepdims=True)
    d = y - mu
    var = jnp.mean(d * d, axis=-1, keepdims=True)
    return d * lax.rsqrt(var + LN_EPS) * g + b


def _params(*semantics):
    return pltpu.CompilerParams(dimension_semantics=semantics, vmem_limit_bytes=VMEM_LIMIT_BYTES)


def _resident(shape):
    nd = len(shape)
    return pl.BlockSpec(shape, lambda *_: (0,) * nd, pipeline_mode=pl.Buffered(1))


W_OFFSET = {"qs": 0, "kvs": D_SWA, "qg": D_SWA + 2 * SWA_KVW}
W_OFFSET["kg"] = W_OFFSET["qg"] + GLA_KW
W_OFFSET["vg"] = W_OFFSET["kg"] + GLA_KW
W_OFFSET["gg"] = W_OFFSET["vg"] + D_GLA
W_OFFSET["gl"] = W_OFFSET["gg"] + D_GLA
W_COLS = W_OFFSET["gl"] + LANES
PREP_ROWS = 256


def _prep_kernel(wt_ref, o_ref):
    gla_w = 2 * GLA_KW + 2 * D_GLA
    swa0 = gla_w + GATE_RANK
    swa_w = D_SWA + 2 * SWA_KVW
    o_ref[:, :swa_w] = wt_ref[swa0:swa0 + swa_w, :].T.astype(BF16)
    o_ref[:, swa_w:swa_w + gla_w] = wt_ref[:gla_w, :].T.astype(BF16)
    gate_rank = jnp.concatenate([wt_ref[gla_w:swa0, :], jnp.zeros((LANES - GATE_RANK, wt_ref.shape[1]), F32)], axis=0)
    o_ref[:, swa_w + gla_w:] = gate_rank.T.astype(BF16)


def _prep_w_in(w_in):
    rows, feats = w_in.shape
    return pl.pallas_call(
        _prep_kernel,
        grid=(rows // PREP_ROWS,),
        in_specs=[pl.BlockSpec((feats, PREP_ROWS), lambda i: (0, i))],
        out_specs=pl.BlockSpec((PREP_ROWS, W_COLS), lambda i: (i, 0)),
        out_shape=jax.ShapeDtypeStruct((rows, W_COLS), BF16),
        compiler_params=_params("parallel"),
        name="w_in_prep",
    )(w_in.T)


def _gla_chunk_head(q_ref, k_ref, v_ref, b_ref, gk_ref, gate_ref, st_ref, o_ref, slot, base, h, c, consts):
    eye, small, small_mask, row_cd, sub8, lane8 = consts
    dk, dv = GLA_DK, GLA_DV
    rows = pl.ds(base, c)
    kcols = pl.ds(h * dk, dk)
    vcols = pl.ds(h * dv, dv)

    def brow(r):
        return jnp.broadcast_to(b_ref[slot, pl.ds(r, 1), kcols], (SUBLANES, dk))

    b = b_ref[slot, rows, kcols]
    qb = q_ref[slot, rows, kcols]
    kb = k_ref[slot, rows, kcols]
    q = qb.astype(F32)
    k = kb.astype(F32)
    v = v_ref[slot, rows, vcols]
    b_last = b_ref[slot, pl.ds(base + c - 1, 1), kcols]

    attn = jnp.where(eye, _nt(qb, kb), 0.0)
    for s in small:
        upper = (row_cd & s) != 0
        if s == 1:
            expo = jnp.where(upper, gk_ref[slot, rows, kcols], 0.0)
        else:
            tiles = []
            for r0 in range(0, c, SUBLANES):
                mids = [brow(base + r0 + o + s - 1) for o in range(0, SUBLANES, 2 * s)]
                m8 = mids[-1]
                for idx in range(len(mids) - 2, -1, -1):
                    m8 = jnp.where(sub8 < (idx + 1) * 2 * s, mids[idx], m8)
                tiles.append(m8)
            expo = (b - jnp.concatenate(tiles, axis=0)) * jnp.where(upper, 1.0, -1.0)
        t = (jnp.where(upper, q, k) * jnp.exp2(expo)).astype(BF16)
        attn = jnp.where(small_mask[s], _nt(t, t), attn)

    slabs = [attn[r:r + SUBLANES, :] for r in range(0, c, SUBLANES)]
    s = SUBLANES
    while 2 * s <= c:
        sel, expo = [], []
        for r0 in range(0, c, 2 * s):
            mid = jnp.concatenate([brow(base + r0 + s - 1)] * (s // SUBLANES), axis=0)
            sel += [k[r0:r0 + s, :], q[r0 + s:r0 + 2 * s, :]]
            expo += [mid - b[r0:r0 + s, :], b[r0 + s:r0 + 2 * s, :] - mid]
        t = (jnp.concatenate(sel, axis=0) * jnp.exp2(jnp.concatenate(expo, axis=0))).astype(BF16)
        g_lvl = _nt(t, t)
        for r0 in range(0, c, 2 * s):
            in_lower = (lane8 >= r0) & (lane8 < r0 + s)
            for r in range(r0 + s, r0 + 2 * s, SUBLANES):
                slabs[r // SUBLANES] = jnp.where(in_lower, g_lvl[r:r + SUBLANES, :], slabs[r // SUBLANES])
        s *= 2
    attn = jnp.concatenate(slabs, axis=0).astype(BF16)

    st = st_ref[h]
    q_in = (q * jnp.exp2(b)).astype(BF16)
    o = _nt(q_in, st.astype(BF16)) + _nn(attn, v)
    k_out = (k * jnp.exp2(b_last - b)).astype(BF16)
    st_ref[h] = st * jnp.exp2(b_last) + _tn(v, k_out)

    o = o * lax.rsqrt(jnp.mean(o * o, axis=-1, keepdims=True) + RMS_EPS)
    o_ref[rows, vcols] = (o * gate_ref[slot, rows, vcols].astype(F32)).astype(o_ref.dtype)


def _gla_consts(c):
    row_cc = lax.broadcasted_iota(jnp.int32, (c, c), 0)
    col_cc = lax.broadcasted_iota(jnp.int32, (c, c), 1)
    eye = row_cc == col_cc
    small = [s for s in (1, 2, 4) if 2 * s <= min(c, SUBLANES)]
    small_mask = {s: ((row_cc & s) != 0) & ((col_cc & s) == 0) & ((row_cc ^ col_cc) < 2 * s) for s in small}
    row_cd = lax.broadcasted_iota(jnp.int32, (c, GLA_DK), 0)
    sub8 = lax.broadcasted_iota(jnp.int32, (SUBLANES, GLA_DK), 0)
    lane8 = lax.broadcasted_iota(jnp.int32, (SUBLANES, c), 1)
    return eye, small, small_mask, row_cd, sub8, lane8


def _swa_block(sink_ref, q_ref, slot, row0, k_prev, k_cur, v_prev, v_cur, prev_penalty, o_ref):
    w = WINDOW
    hd = SWA_HEAD_DIM
    rows = pl.ds(row0, w)
    kf = jnp.concatenate([k_prev, k_cur], axis=0).astype(F32)
    vf = jnp.concatenate([v_prev, v_cur], axis=0).astype(F32)
    low_kv = lax.broadcasted_iota(jnp.int32, (2 * w, LANES), 1) < hd
    qi = lax.broadcasted_iota(jnp.int32, (w, w), 0)
    kj = lax.broadcasted_iota(jnp.int32, (w, w), 1)
    from_prev = kj > qi
    low_o = lax.broadcasted_iota(jnp.int32, (w, LANES), 1) < hd
    ones_lo = low_kv.astype(BF16)
    ones_hi = (~low_kv).astype(BF16)
    pairs = SWA_GROUP // 2
    for g in range(SWA_KV_HEADS):
        k_rot = kf if g == 0 else pltpu.roll(kf, hd, axis=1)
        v_rot = vf if g == 0 else pltpu.roll(vf, hd, axis=1)
        k_lo = jnp.where(low_kv, k_rot, 0.0)
        v_lo = jnp.where(low_kv, v_rot, 0.0)
        k_hi = pltpu.roll(k_lo, hd, axis=1).astype(BF16)
        v_hi = pltpu.roll(v_lo, hd, axis=1).astype(BF16)
        vw = (jnp.concatenate([v_lo.astype(BF16), ones_lo], axis=1),
              jnp.concatenate([v_hi, ones_hi], axis=1))
        kk = jnp.concatenate([k_lo.astype(BF16), k_hi], axis=0)
        qst = jnp.concatenate(
            [q_ref[slot, rows, pl.ds((g * pairs + p) * LANES, LANES)] for p in range(pairs)], axis=0)
        s_all = _nt(qst, kk)
        acc = None
        sink_terms = []
        for par in range(2):
            probs, terms = [], []
            for p in range(pairs):
                sink = sink_ref[2 * (g * pairs + p) + par] * LOG2_E
                s2 = s_all[p * w:(p + 1) * w, 2 * w * par:2 * w * (par + 1)]
                s_prev = s2[:, :w] if prev_penalty is None else s2[:, :w] + prev_penalty
                s = jnp.where(from_prev, s_prev, s2[:, w:])
                m = jnp.maximum(jnp.max(s, axis=-1, keepdims=True), sink)
                e = jnp.exp2(s - m)
                probs.append(jnp.concatenate([jnp.where(from_prev, e, 0.0).astype(BF16),
                                              jnp.where(from_prev, 0.0, e).astype(BF16)], axis=1))
                terms.append(jnp.exp2(sink - m))
            part = _nn(jnp.concatenate(probs, axis=0), vw[par])
            acc = part if acc is None else acc + part
            sink_terms.append(terms)
        for p in range(pairs):
            blk = acc[p * w:(p + 1) * w, :]
            denom = blk[:, LANES:] + jnp.where(low_o, sink_terms[0][p], sink_terms[1][p])
            o_ref[rows, pl.ds((g * pairs + p) * LANES, LANES)] = (blk[:, :LANES] / denom).astype(o_ref.dtype)


def _mixer_kernel(sink_ref, x_ref, w_ref, wgk2, bgk, nw, wo_f, wu_f, wd_f,
                  gla_o, swa_o, wo_b, wu_b, wd_b,
                  xb_s, qs_s, kvs_s, qg_s, kg_s, vg_s, gate_s, gk_s, b_s, st_s, kvc_s,
                  *, tiles_per_seq, chunk):
    for src, dst in ((wo_f, wo_b), (wu_f, wu_b), (wd_f, wd_b)):
        dst[...] = src[...].astype(BF16)

    i = pl.program_id(0)
    ws = lax.rem(i, 2)
    rs = 1 - ws
    seq_start = lax.rem(i + tiles_per_seq - 1, tiles_per_seq) == 0
    tm = x_ref.shape[0]
    proj_scratch = (qs_s, kvs_s, qg_s, kg_s, vg_s, gate_s, gk_s, b_s)

    @pl.when(i == 0)
    def _():
        for s in proj_scratch:
            s[1] = jnp.zeros(s.shape[1:], s.dtype)
        kvc_s[...] = jnp.zeros_like(kvc_s)

    @pl.when(jnp.logical_or(i == 0, seq_start))
    def _():
        st_s[...] = jnp.zeros_like(st_s)

    consts = _gla_consts(chunk)
    row = lax.broadcasted_iota(jnp.int32, (chunk, chunk), 0)
    col = lax.broadcasted_iota(jnp.int32, (chunk, chunk), 1)
    tril = (col <= row).astype(BF16)

    def proj(group, c0, width):
        w0 = W_OFFSET[group] + c0
        return _nn(xb_s[...], w_ref[:, w0:w0 + width])

    def plain(group, dst, scale, c0):
        y = proj(group, c0, MXU_COLS)
        dst[ws, :, c0:c0 + MXU_COLS] = (y if scale is None else y * scale).astype(BF16)

    def out_gate(c0):
        g = proj("gg", c0, MXU_COLS)
        gate_s[ws, :, c0:c0 + MXU_COLS] = (nw[:, c0:c0 + MXU_COLS] * g / (1.0 + jnp.exp(-g))).astype(BF16)

    def forget_gate():
        gl = proj("gl", 0, LANES).astype(BF16)
        z = _nn(gl, wgk2[...]) + bgk[...]
        gk = (jnp.minimum(z, 0.0) - jnp.log1p(jnp.exp(-jnp.abs(z)))) * (LOG2_E / GATE_TAU)
        gk_s[ws] = gk
        gk_hi = gk.astype(BF16)
        gk_lo = (gk - gk_hi.astype(F32)).astype(BF16)
        for r0 in range(0, tm, chunk):
            b_s[ws, r0:r0 + chunk, :] = (_nn(tril, gk_hi[r0:r0 + chunk, :]) + _nn(tril, gk_lo[r0:r0 + chunk, :]))

    def swa(j):
        r0 = j * WINDOW
        if j == 0:
            k_prev, v_prev = kvc_s[:, :SWA_KVW], kvc_s[:, SWA_KVW:]
            penalty = jnp.where(seq_start, NEG_BIG, 0.0)
        else:
            k_prev = kvs_s[rs, r0 - WINDOW:r0, :SWA_KVW]
            v_prev = kvs_s[rs, r0 - WINDOW:r0, SWA_KVW:]
            penalty = None
        _swa_block(sink_ref, qs_s, rs, r0, k_prev, kvs_s[rs, r0:r0 + WINDOW, :SWA_KVW],
                   v_prev, kvs_s[rs, r0:r0 + WINDOW, SWA_KVW:], penalty, swa_o)

    def gla(ci):
        for h in range(GLA_HEADS):
            _gla_chunk_head(qg_s, kg_s, vg_s, b_s, gk_s, gate_s, st_s, gla_o, rs, ci * chunk, h, chunk, consts)

    xb_s[...] = x_ref[...].astype(BF16)
    n_swa = tm // WINDOW
    n_gla = tm // chunk
    mix = []
    for n in range(max(n_swa, n_gla)):
        mix += [functools.partial(swa, n)] if n < n_swa else []
        mix += [functools.partial(gla, n)] if n < n_gla else []
    projs = []
    for group, dst, scale in (("qs", qs_s, SWA_HEAD_DIM ** -0.5 * LOG2_E), ("kvs", kvs_s, None),
                              ("qg", qg_s, GLA_DK ** -0.5), ("kg", kg_s, None), ("vg", vg_s, None)):
        projs += [functools.partial(plain, group, dst, scale, c0) for c0 in range(0, dst.shape[2], MXU_COLS)]
    projs += [functools.partial(out_gate, c0) for c0 in range(0, D_GLA, MXU_COLS)] + [forget_gate]
    for n, m in enumerate(mix):
        for p in projs[n * len(projs) // len(mix):(n + 1) * len(projs) // len(mix)]:
            p()
        m()
    kvc_s[...] = kvs_s[rs, tm - WINDOW:tm, :]


def _mixer(x2, w_in, w_gk2, b_gk, norm_w, sinks, seq, w_out, w_up, w_down):
    n = x2.shape[0]
    tm = min(MIX_ROWS, seq)
    chunk = min(GLA_CHUNK, tm)
    n_tiles = n // tm
    wgk2 = jnp.pad(w_gk2.astype(BF16), ((0, LANES - GATE_RANK), (0, 0)))
    bgk = b_gk.reshape(1, GLA_KW).astype(F32)
    nw = jnp.tile(norm_w.astype(F32), GLA_HEADS).reshape(1, D_GLA)
    weights = (_prep_w_in(w_in), wgk2, bgk, nw)
    slot = lambda d, t: pltpu.VMEM((2, tm, d), t)
    out_spec = pl.BlockSpec((tm, D_GLA), lambda i: (jnp.maximum(i - 1, 0), 0))
    blk = lambda i: jnp.minimum(i, n_tiles - 1)
    assert D_MODEL % n_tiles == 0 and D_FF % n_tiles == 0
    cast_specs = [pl.BlockSpec((D_MODEL // n_tiles, D_MODEL), lambda i: (blk(i), 0)),
                  pl.BlockSpec((D_MODEL, D_FF // n_tiles), lambda i: (0, blk(i))),
                  pl.BlockSpec((D_FF // n_tiles, D_MODEL), lambda i: (blk(i), 0))]
    return pl.pallas_call(
        functools.partial(_mixer_kernel, tiles_per_seq=seq // tm, chunk=chunk),
        grid=(n_tiles + 1,),
        in_specs=[pl.BlockSpec(memory_space=pltpu.SMEM),
                  pl.BlockSpec((tm, D_MODEL), lambda i: (jnp.minimum(i, n_tiles - 1), 0))]
                 + [_resident(w.shape) for w in weights] + cast_specs,
        out_specs=[out_spec, out_spec] + cast_specs,
        out_shape=[jax.ShapeDtypeStruct((n, D_GLA), BF16), jax.ShapeDtypeStruct((n, D_SWA), BF16)]
                  + [jax.ShapeDtypeStruct(w.shape, BF16) for w in (w_out, w_up, w_down)],
        scratch_shapes=[pltpu.VMEM((tm, D_MODEL), BF16),
                        slot(D_SWA, BF16), slot(2 * SWA_KVW, BF16), slot(GLA_KW, BF16), slot(GLA_KW, BF16),
                        slot(D_GLA, BF16), slot(D_GLA, BF16), slot(GLA_KW, F32), slot(GLA_KW, F32),
                        pltpu.VMEM((GLA_HEADS, GLA_DV, GLA_DK), F32),
                        pltpu.VMEM((WINDOW, 2 * SWA_KVW), BF16)],
        compiler_params=_params("arbitrary"),
        name="token_mixer",
    )(sinks.astype(F32), x2, *weights, w_out, w_up, w_down)


def _channel_kernel(f, last, gla_ref, swa_ref, x_ref, wo_ref, g1_ref, b1_ref, wu_ref, wd_ref, g2_ref, b2_ref,
                    o_ref, xb_ref):
    tm = x_ref.shape[0]

    def hidden():
        h = jnp.maximum(_nn(xb_ref[...], wu_ref[...]), 0.0)
        return (h * h).astype(BF16)

    subs = [pl.ds(r0, LN_SUB_ROWS) for r0 in range(0, tm, LN_SUB_ROWS)]

    @pl.when(f == 0)
    def _():
        for n in range(len(subs) + 1):
            if n < len(subs):
                rows = subs[n]
                o_ref[rows, :] = (ALPHA * x_ref[rows, :] + _nn(gla_ref[rows, :], wo_ref[:D_GLA, :])
                                  + _nn(swa_ref[rows, :], wo_ref[D_GLA:, :]))
            if n >= 1:
                rows = subs[n - 1]
                x1 = _layer_norm(o_ref[rows, :], g1_ref[...], b1_ref[...])
                xb_ref[rows, :] = x1.astype(BF16)
                o_ref[rows, :] = ALPHA * x1
        o_ref[...] += _nn(hidden(), wd_ref[...])

    @pl.when(jnp.logical_and(f > 0, f < last))
    def _():
        o_ref[...] += _nn(hidden(), wd_ref[...])

    @pl.when(f == last)
    def _():
        h = hidden()
        for n in range(len(subs) + 1):
            if n < len(subs):
                o_ref[subs[n], :] += _nn(h[n * LN_SUB_ROWS:(n + 1) * LN_SUB_ROWS, :], wd_ref[...])
            if n >= 1:
                rows = subs[n - 1]
                o_ref[rows, :] = _layer_norm(o_ref[rows, :], g2_ref[...], b2_ref[...])


def _channel_mixer(gla, swa, x2, w_out, ln1_g, ln1_b, w_up, w_down, ln2_g, ln2_b):
    n = x2.shape[0]
    tm = min(MLP_ROWS, n)
    tf = MLP_FF_TILE
    assert D_FF // tf >= 2
    row = lambda d: pl.BlockSpec((tm, d), lambda i, f: (i, 0))
    vec = lambda: _resident((1, D_MODEL))
    as_row = lambda t: t.reshape(1, D_MODEL)
    grid = (n // tm, D_FF // tf)
    in_specs = [row(D_GLA), row(D_SWA), row(D_MODEL), _resident(w_out.shape), vec(), vec(),
                pl.BlockSpec((D_MODEL, tf), lambda i, f: (0, f)),
                pl.BlockSpec((tf, D_MODEL), lambda i, f: (f, 0)), vec(), vec()]

    def outer(*refs):
        *operands, xb_ref, step_ref = refs
        step_ref[0] = 0

        def body(*blocks):
            step = step_ref[0]
            _channel_kernel(lax.rem(step, grid[1]), grid[1] - 1, *blocks, xb_ref)
            step_ref[0] = step + 1

        pltpu.emit_pipeline(body, grid=grid, in_specs=in_specs, out_specs=[row(D_MODEL)])(*operands)

    hbm = pl.BlockSpec(memory_space=pl.ANY)
    return pl.pallas_call(
        outer,
        in_specs=[hbm] * len(in_specs),
        out_specs=hbm,
        out_shape=jax.ShapeDtypeStruct((n, D_MODEL), F32),
        scratch_shapes=[pltpu.VMEM((tm, D_MODEL), BF16), pltpu.SMEM((1,), jnp.int32)],
        compiler_params=pltpu.CompilerParams(vmem_limit_bytes=VMEM_LIMIT_BYTES),
        name="channel_mixer",
    )(gla, swa, x2, w_out, as_row(ln1_g), as_row(ln1_b), w_up, w_down,
      as_row(ln2_g), as_row(ln2_b))


def kernel(x, w_in, w_gk2, b_gk, gla_norm_w, swa_sinks, w_out, ln1_g, ln1_b, w_up, w_down, ln2_g, ln2_b):
    bsz, seq, _ = x.shape
    n = bsz * seq
    x2 = x.reshape(n, D_MODEL)
    for l in range(DEPTH):
        gla, swa, wo, wu, wd = _mixer(x2, w_in[l], w_gk2[l], b_gk[l], gla_norm_w[l], swa_sinks[l], seq,
                                      w_out[l], w_up[l], w_down[l])
        x2 = _channel_mixer(gla, swa, x2, wo, ln1_g[l], ln1_b[l], wu, wd, ln2_g[l], ln2_b[l])
    return x2.reshape(bsz, seq, D_MODEL)
```

```python
import functools

import jax
import jax.numpy as jnp
from jax import lax
from jax.experimental import pallas as pl
from jax.experimental.pallas import tpu as pltpu

F32 = jnp.float32
BF16 = jnp.bfloat16

D_MODEL = 2048
D_GLA = 1024
GLA_HEADS = 4
GLA_DK = 128
GLA_DV = 256
GLA_KW = GLA_HEADS * GLA_DK
GATE_RANK = 16
GATE_TAU = 16.0
SWA_HEAD_DIM = 64
SWA_Q_HEADS = 16
SWA_KV_HEADS = 2
SWA_GROUP = SWA_Q_HEADS // SWA_KV_HEADS
D_SWA = SWA_Q_HEADS * SWA_HEAD_DIM
SWA_KVW = SWA_KV_HEADS * SWA_HEAD_DIM
WINDOW = 128
D_FF = 4 * D_MODEL
SPLITS = (GLA_KW, GLA_KW, D_GLA, D_GLA, GATE_RANK, D_SWA, SWA_KVW, SWA_KVW)
DEPTH = 1
ALPHA = (2 * DEPTH) ** 0.25
LN_EPS = 1e-5
RMS_EPS = 1e-5

LANES = 128
SUBLANES = 8
MXU_COLS = 256
VMEM_LIMIT_BYTES = 56 * 1024 * 1024

MIX_ROWS = 256
LN_SUB_ROWS = 128
GLA_CHUNK = 128
MLP_ROWS = 512
MLP_FF_TILE = 1024
NEG_BIG = -1e30
LOG2_E = 1.4426950408889634

assert MIX_ROWS % GLA_CHUNK == 0 and MIX_ROWS % WINDOW == 0

def _nn(a, b):
    return lax.dot_general(a, b, (((1,), (0,)), ((), ())), preferred_element_type=F32)


def _nt(a, b):
    return lax.dot_general(a, b, (((1,), (1,)), ((), ())), preferred_element_type=F32)


def _tn(a, b):
    return lax.dot_general(a, b, (((0,), (0,)), ((), ())), preferred_element_type=F32)


def _layer_norm(y, g, b):
    mu = jnp.mean(y, axis=-1, keepdims=True)
    d = y - mu
    var = jnp.mean(d * d, axis=-1, keepdims=True)
    return d * lax.rsqrt(var + LN_EPS) * g + b


def _params(*semantics):
    return pltpu.CompilerParams(dimension_semantics=semantics, vmem_limit_bytes=VMEM_LIMIT_BYTES)


def _resident(shape):
    nd = len(shape)
    return pl.BlockSpec(shape, lambda *_: (0,) * nd, pipeline_mode=pl.Buffered(1))


W_OFFSET = {"qs": 0, "kvs": D_SWA, "qg": D_SWA + 2 * SWA_KVW}
W_OFFSET["kg"] = W_OFFSET["qg"] + GLA_KW
W_OFFSET["vg"] = W_OFFSET["kg"] + GLA_KW
W_OFFSET["gg"] = W_OFFSET["vg"] + D_GLA
W_OFFSET["gl"] = W_OFFSET["gg"] + D_GLA
W_COLS = W_OFFSET["gl"] + LANES
PREP_ROWS = 256


def _prep_kernel(wt_ref, o_ref):
    gla_w = 2 * GLA_KW + 2 * D_GLA
    swa0 = gla_w + GATE_RANK
    swa_w = D_SWA + 2 * SWA_KVW
    o_ref[:, :swa_w] = wt_ref[swa0:swa0 + swa_w, :].T.astype(BF16)
    o_ref[:, swa_w:swa_w + gla_w] = wt_ref[:gla_w, :].T.astype(BF16)
    gate_rank = jnp.concatenate([wt_ref[gla_w:swa0, :], jnp.zeros((LANES - GATE_RANK, wt_ref.shape[1]), F32)], axis=0)
    o_ref[:, swa_w + gla_w:] = gate_rank.T.astype(BF16)


def _prep_w_in(w_in):
    rows, feats = w_in.shape
    return pl.pallas_call(
        _prep_kernel,
        grid=(rows // PREP_ROWS,),
        in_specs=[pl.BlockSpec((feats, PREP_ROWS), lambda i: (0, i))],
        out_specs=pl.BlockSpec((PREP_ROWS, W_COLS), lambda i: (i, 0)),
        out_shape=jax.ShapeDtypeStruct((rows, W_COLS), BF16),
        compiler_params=_params("parallel"),
        name="w_in_prep",
    )(w_in.T)


def _gla_chunk_head(q_ref, k_ref, v_ref, b_ref, gk_ref, gate_ref, st_ref, o_ref, slot, base, h, c, consts):
    eye, small, small_mask, row_cd, sub8, lane8 = consts
    dk, dv = GLA_DK, GLA_DV
    rows = pl.ds(base, c)
    kcols = pl.ds(h * dk, dk)
    vcols = pl.ds(h * dv, dv)

    def brow(r):
        return jnp.broadcast_to(b_ref[slot, pl.ds(r, 1), kcols], (SUBLANES, dk))

    b = b_ref[slot, rows, kcols]
    qb = q_ref[slot, rows, kcols]
    kb = k_ref[slot, rows, kcols]
    q = qb.astype(F32)
    k = kb.astype(F32)
    v = v_ref[slot, rows, vcols]
    b_last = b_ref[slot, pl.ds(base + c - 1, 1), kcols]

    attn = jnp.where(eye, _nt(qb, kb), 0.0)
    for s in small:
        upper = (row_cd & s) != 0
        if s == 1:
            expo = jnp.where(upper, gk_ref[slot, rows, kcols], 0.0)
        else:
            tiles = []
            for r0 in range(0, c, SUBLANES):
                mids = [brow(base + r0 + o + s - 1) for o in range(0, SUBLANES, 2 * s)]
                m8 = mids[-1]
                for idx in range(len(mids) - 2, -1, -1):
                    m8 = jnp.where(sub8 < (idx + 1) * 2 * s, mids[idx], m8)
                tiles.append(m8)
            expo = (b - jnp.concatenate(tiles, axis=0)) * jnp.where(upper, 1.0, -1.0)
        t = (jnp.where(upper, q, k) * jnp.exp2(expo)).astype(BF16)
        attn = jnp.where(small_mask[s], _nt(t, t), attn)

    slabs = [attn[r:r + SUBLANES, :] for r in range(0, c, SUBLANES)]
    s = SUBLANES
    while 2 * s <= c:
        sel, expo = [], []
        for r0 in range(0, c, 2 * s):
            mid = jnp.concatenate([brow(base + r0 + s - 1)] * (s // SUBLANES), axis=0)
            sel += [k[r0:r0 + s, :], q[r0 + s:r0 + 2 * s, :]]
            expo += [mid - b[r0:r0 + s, :], b[r0 + s:r0 + 2 * s, :] - mid]
        t = (jnp.concatenate(sel, axis=0) * jnp.exp2(jnp.concatenate(expo, axis=0))).astype(BF16)
        g_lvl = _nt(t, t)
        for r0 in range(0, c, 2 * s):
            in_lower = (lane8 >= r0) & (lane8 < r0 + s)
            for r in range(r0 + s, r0 + 2 * s, SUBLANES):
                slabs[r // SUBLANES] = jnp.where(in_lower, g_lvl[r:r + SUBLANES, :], slabs[r // SUBLANES])
        s *= 2
    attn = jnp.concatenate(slabs, axis=0).astype(BF16)

    st = st_ref[h]
    q_in = (q * jnp.exp2(b)).astype(BF16)
    o = _nt(q_in, st.astype(BF16)) + _nn(attn, v)
    k_out = (k * jnp.exp2(b_last - b)).astype(BF16)
    st_ref[h] = st * jnp.exp2(b_last) + _tn(v, k_out)

    o = o * lax.rsqrt(jnp.mean(o * o, axis=-1, keepdims=True) + RMS_EPS)
    o_ref[rows, vcols] = (o * gate_ref[slot, rows, vcols].astype(F32)).astype(o_ref.dtype)


def _gla_consts(c):
    row_cc = lax.broadcasted_iota(jnp.int32, (c, c), 0)
    col_cc = lax.broadcasted_iota(jnp.int32, (c, c), 1)
    eye = row_cc == col_cc
    small = [s for s in (1, 2, 4) if 2 * s <= min(c, SUBLANES)]
    small_mask = {s: ((row_cc & s) != 0) & ((col_cc & s) == 0) & ((row_cc ^ col_cc) < 2 * s) for s in small}
    row_cd = lax.broadcasted_iota(jnp.int32, (c, GLA_DK), 0)
    sub8 = lax.broadcasted_iota(jnp.int32, (SUBLANES, GLA_DK), 0)
    lane8 = lax.broadcasted_iota(jnp.int32, (SUBLANES, c), 1)
    return eye, small, small_mask, row_cd, sub8, lane8


def _swa_block(sink_ref, q_ref, slot, row0, k_prev, k_cur, v_prev, v_cur, prev_penalty, o_ref):
    w = WINDOW
    hd = SWA_HEAD_DIM
    rows = pl.ds(row0, w)
    kf = jnp.concatenate([k_prev, k_cur], axis=0).astype(F32)
    vf = jnp.concatenate([v_prev, v_cur], axis=0).astype(F32)
    low_kv = lax.broadcasted_iota(jnp.int32, (2 * w, LANES), 1) < hd
    qi = lax.broadcasted_iota(jnp.int32, (w, w), 0)
    kj = lax.broadcasted_iota(jnp.int32, (w, w), 1)
    from_prev = kj > qi
    low_o = lax.broadcasted_iota(jnp.int32, (w, LANES), 1) < hd
    ones_lo = low_kv.astype(BF16)
    ones_hi = (~low_kv).astype(BF16)
    pairs = SWA_GROUP // 2
    for g in range(SWA_KV_HEADS):
        k_rot = kf if g == 0 else pltpu.roll(kf, hd, axis=1)
        v_rot = vf if g == 0 else pltpu.roll(vf, hd, axis=1)
        k_lo = jnp.where(low_kv, k_rot, 0.0)
        v_lo = jnp.where(low_kv, v_rot, 0.0)
        k_hi = pltpu.roll(k_lo, hd, axis=1).astype(BF16)
        v_hi = pltpu.roll(v_lo, hd, axis=1).astype(BF16)
        vw = (jnp.concatenate([v_lo.astype(BF16), ones_lo], axis=1),
              jnp.concatenate([v_hi, ones_hi], axis=1))
        kk = jnp.concatenate([k_lo.astype(BF16), k_hi], axis=0)
        qst = jnp.concatenate(
            [q_ref[slot, rows, pl.ds((g * pairs + p) * LANES, LANES)] for p in range(pairs)], axis=0)
        s_all = _nt(qst, kk)
        acc = None
        sink_terms = []
        for par in range(2):
            probs, terms = [], []
            for p in range(pairs):
                sink = sink_ref[2 * (g * pairs + p) + par] * LOG2_E
                s2 = s_all[p * w:(p + 1) * w, 2 * w * par:2 * w * (par + 1)]
                s_prev = s2[:, :w] if prev_penalty is None else s2[:, :w] + prev_penalty
                s = jnp.where(from_prev, s_prev, s2[:, w:])
                m = jnp.maximum(jnp.max(s, axis=-1, keepdims=True), sink)
                e = jnp.exp2(s - m)
                probs.append(jnp.concatenate([jnp.where(from_prev, e, 0.0).astype(BF16),
                                              jnp.where(from_prev, 0.0, e).astype(BF16)], axis=1))
                terms.append(jnp.exp2(sink - m))
            part = _nn(jnp.concatenate(probs, axis=0), vw[par])
            acc = part if acc is None else acc + part
            sink_terms.append(terms)
        for p in range(pairs):
            blk = acc[p * w:(p + 1) * w, :]
            denom = blk[:, LANES:] + jnp.where(low_o, sink_terms[0][p], sink_terms[1][p])
            o_ref[rows, pl.ds((g * pairs + p) * LANES, LANES)] = (blk[:, :LANES] / denom).astype(o_ref.dtype)


def _mixer_kernel(i, sink_ref, x_ref, w_ref, wgk2, bgk, nw, wo_f, wu_f, wd_f,
                  gla_o, swa_o, wo_b, wu_b, wd_b,
                  xb_s, qs_s, kvs_s, qg_s, kg_s, vg_s, gate_s, gk_s, b_s, st_s, kvc_s,
                  *, tiles_per_seq, chunk):
    for src, dst in ((wo_f, wo_b), (wu_f, wu_b), (wd_f, wd_b)):
        dst[...] = src[...].astype(BF16)

    ws = lax.rem(i, 2)
    rs = 1 - ws
    seq_start = lax.rem(i + tiles_per_seq - 1, tiles_per_seq) == 0
    tm = x_ref.shape[0]
    proj_scratch = (qs_s, kvs_s, qg_s, kg_s, vg_s, gate_s, gk_s, b_s)

    @pl.when(i == 0)
    def _():
        for s in proj_scratch:
            s[1] = jnp.zeros(s.shape[1:], s.dtype)
        kvc_s[...] = jnp.zeros_like(kvc_s)

    @pl.when(jnp.logical_or(i == 0, seq_start))
    def _():
        st_s[...] = jnp.zeros_like(st_s)

    consts = _gla_consts(chunk)
    row = lax.broadcasted_iota(jnp.int32, (chunk, chunk), 0)
    col = lax.broadcasted_iota(jnp.int32, (chunk, chunk), 1)
    tril = (col <= row).astype(BF16)

    def proj(group, c0, width):
        w0 = W_OFFSET[group] + c0
        return _nn(xb_s[...], w_ref[:, w0:w0 + width])

    def plain(group, dst, scale, c0):
        y = proj(group, c0, MXU_COLS)
        dst[ws, :, c0:c0 + MXU_COLS] = (y if scale is None else y * scale).astype(BF16)

    def out_gate(c0):
        g = proj("gg", c0, MXU_COLS)
        gate_s[ws, :, c0:c0 + MXU_COLS] = (nw[:, c0:c0 + MXU_COLS] * g / (1.0 + jnp.exp(-g))).astype(BF16)

    def forget_gate():
        gl = proj("gl", 0, LANES).astype(BF16)
        z = _nn(gl, wgk2[...]) + bgk[...]
        gk = (jnp.minimum(z, 0.0) - jnp.log1p(jnp.exp(-jnp.abs(z)))) * (LOG2_E / GATE_TAU)
        gk_s[ws] = gk
        gk_hi = gk.astype(BF16)
        gk_lo = (gk - gk_hi.astype(F32)).astype(BF16)
        for r0 in range(0, tm, chunk):
            b_s[ws, r0:r0 + chunk, :] = (_nn(tril, gk_hi[r0:r0 + chunk, :]) + _nn(tril, gk_lo[r0:r0 + chunk, :]))

    def swa(j):
        r0 = j * WINDOW
        if j == 0:
            k_prev, v_prev = kvc_s[:, :SWA_KVW], kvc_s[:, SWA_KVW:]
            penalty = jnp.where(seq_start, NEG_BIG, 0.0)
        else:
            k_prev = kvs_s[rs, r0 - WINDOW:r0, :SWA_KVW]
            v_prev = kvs_s[rs, r0 - WINDOW:r0, SWA_KVW:]
            penalty = None
        _swa_block(sink_ref, qs_s, rs, r0, k_prev, kvs_s[rs, r0:r0 + WINDOW, :SWA_KVW],
                   v_prev, kvs_s[rs, r0:r0 + WINDOW, SWA_KVW:], penalty, swa_o)

    def gla(ci):
        for h in range(GLA_HEADS):
            _gla_chunk_head(qg_s, kg_s, vg_s, b_s, gk_s, gate_s, st_s, gla_o, rs, ci * chunk, h, chunk, consts)

    xb_s[...] = x_ref[...].astype(BF16)
    n_swa = tm // WINDOW
    n_gla = tm // chunk
    mix = []
    for n in range(max(n_swa, n_gla)):
        mix += [functools.partial(swa, n)] if n < n_swa else []
        mix += [functools.partial(gla, n)] if n < n_gla else []
    projs = []
    for group, dst, scale in (("qs", qs_s, SWA_HEAD_DIM ** -0.5 * LOG2_E), ("kvs", kvs_s, None),
                              ("qg", qg_s, GLA_DK ** -0.5), ("kg", kg_s, None), ("vg", vg_s, None)):
        projs += [functools.partial(plain, group, dst, scale, c0) for c0 in range(0, dst.shape[2], MXU_COLS)]
    projs += [functools.partial(out_gate, c0) for c0 in range(0, D_GLA, MXU_COLS)] + [forget_gate]
    for n, m in enumerate(mix):
        for p in projs[n * len(projs) // len(mix):(n + 1) * len(projs) // len(mix)]:
            p()
        m()
    kvc_s[...] = kvs_s[rs, tm - WINDOW:tm, :]


def _mixer(x2, w_in, w_gk2, b_gk, norm_w, sinks, seq, w_out, w_up, w_down):
    n = x2.shape[0]
    tm = min(MIX_ROWS, seq)
    chunk = min(GLA_CHUNK, tm)
    n_tiles = n // tm
    wgk2 = jnp.pad(w_gk2.astype(BF16), ((0, LANES - GATE_RANK), (0, 0)))
    bgk = b_gk.reshape(1, GLA_KW).astype(F32)
    nw = jnp.tile(norm_w.astype(F32), GLA_HEADS).reshape(1, D_GLA)
    weights = (_prep_w_in(w_in), wgk2, bgk, nw)
    slot = lambda d, t: pltpu.VMEM((2, tm, d), t)
    out_spec = pl.BlockSpec((tm, D_GLA), lambda i: (jnp.maximum(i - 1, 0), 0))
    blk = lambda i: jnp.minimum(i, n_tiles - 1)
    assert D_MODEL % n_tiles == 0 and D_FF % n_tiles == 0
    cast_specs = [pl.BlockSpec((D_MODEL // n_tiles, D_MODEL), lambda i: (blk(i), 0)),
                  pl.BlockSpec((D_MODEL, D_FF // n_tiles), lambda i: (0, blk(i))),
                  pl.BlockSpec((D_FF // n_tiles, D_MODEL), lambda i: (blk(i), 0))]
    in_specs = ([pl.BlockSpec((tm, D_MODEL), lambda i: (jnp.minimum(i, n_tiles - 1), 0))]
                + [_resident(w.shape) for w in weights] + cast_specs)
    out_specs = [out_spec, out_spec] + cast_specs
    n_in = len(in_specs)

    def outer(sink_ref, *refs):
        operands, scratch, step_ref = refs[:n_in + len(out_specs)], refs[n_in + len(out_specs):-1], refs[-1]
        step_ref[0] = 0

        def body(*blocks):
            step = step_ref[0]
            _mixer_kernel(step, sink_ref, *blocks, *scratch, tiles_per_seq=seq // tm, chunk=chunk)
            step_ref[0] = step + 1

        pltpu.emit_pipeline(body, grid=(n_tiles + 1,), in_specs=in_specs, out_specs=out_specs)(*operands)

    hbm = pl.BlockSpec(memory_space=pl.ANY)
    return pl.pallas_call(
        outer,
        in_specs=[pl.BlockSpec(memory_space=pltpu.SMEM)] + [hbm] * n_in,
        out_specs=[hbm] * len(out_specs),
        out_shape=[jax.ShapeDtypeStruct((n, D_GLA), BF16), jax.ShapeDtypeStruct((n, D_SWA), BF16)]
                  + [jax.ShapeDtypeStruct(w.shape, BF16) for w in (w_out, w_up, w_down)],
        scratch_shapes=[pltpu.VMEM((tm, D_MODEL), BF16),
                        slot(D_SWA, BF16), slot(2 * SWA_KVW, BF16), slot(GLA_KW, BF16), slot(GLA_KW, BF16),
                        slot(D_GLA, BF16), slot(D_GLA, BF16), slot(GLA_KW, F32), slot(GLA_KW, F32),
                        pltpu.VMEM((GLA_HEADS, GLA_DV, GLA_DK), F32),
                        pltpu.VMEM((WINDOW, 2 * SWA_KVW), BF16),
                        pltpu.SMEM((1,), jnp.int32)],
        compiler_params=pltpu.CompilerParams(vmem_limit_bytes=VMEM_LIMIT_BYTES),
        name="token_mixer",
    )(sinks.astype(F32), x2, *weights, w_out, w_up, w_down)


def _channel_kernel(gla_ref, swa_ref, x_ref, wo_ref, g1_ref, b1_ref, wu_ref, wd_ref, g2_ref, b2_ref,
                    o_ref, xb_ref):
    f = pl.program_id(1)
    last = pl.num_programs(1) - 1
    tm = x_ref.shape[0]

    def hidden():
        h = jnp.maximum(_nn(xb_ref[...], wu_ref[...]), 0.0)
        return (h * h).astype(BF16)

    subs = [pl.ds(r0, LN_SUB_ROWS) for r0 in range(0, tm, LN_SUB_ROWS)]

    @pl.when(f == 0)
    def _():
        for n in range(len(subs) + 1):
            if n < len(subs):
                rows = subs[n]
                o_ref[rows, :] = (ALPHA * x_ref[rows, :] + _nn(gla_ref[rows, :], wo_ref[:D_GLA, :])
                                  + _nn(swa_ref[rows, :], wo_ref[D_GLA:, :]))
            if n >= 1:
                rows = subs[n - 1]
                x1 = _layer_norm(o_ref[rows, :], g1_ref[...], b1_ref[...])
                xb_ref[rows, :] = x1.astype(BF16)
                o_ref[rows, :] = ALPHA * x1
        o_ref[...] += _nn(hidden(), wd_ref[...])

    @pl.when(jnp.logical_and(f > 0, f < last))
    def _():
        o_ref[...] += _nn(hidden(), wd_ref[...])

    @pl.when(f == last)
    def _():
        h = hidden()
        for n in range(len(subs) + 1):
            if n < len(subs):
                o_ref[subs[n], :] += _nn(h[n * LN_SUB_ROWS:(n + 1) * LN_SUB_ROWS, :], wd_ref[...])
            if n >= 1:
                rows = subs[n - 1]
                o_ref[rows, :] = _layer_norm(o_ref[rows, :], g2_ref[...], b2_ref[...])


def _channel_mixer(gla, swa, x2, w_out, ln1_g, ln1_b, w_up, w_down, ln2_g, ln2_b):
    n = x2.shape[0]
    tm = min(MLP_ROWS, n)
    tf = MLP_FF_TILE
    assert D_FF // tf >= 2
    row = lambda d: pl.BlockSpec((tm, d), lambda i, f: (i, 0))
    vec = lambda: _resident((1, D_MODEL))
    as_row = lambda t: t.reshape(1, D_MODEL)
    return pl.pallas_call(
        _channel_kernel,
        grid=(n // tm, D_FF // tf),
        in_specs=[row(D_GLA), row(D_SWA), row(D_MODEL), _resident(w_out.shape), vec(), vec(),
                  pl.BlockSpec((D_MODEL, tf), lambda i, f: (0, f)),
                  pl.BlockSpec((tf, D_MODEL), lambda i, f: (f, 0)), vec(), vec()],
        out_specs=row(D_MODEL),
        out_shape=jax.ShapeDtypeStruct((n, D_MODEL), F32),
        scratch_shapes=[pltpu.VMEM((tm, D_MODEL), BF16)],
        compiler_params=_params("parallel", "arbitrary"),
        name="channel_mixer",
    )(gla, swa, x2, w_out, as_row(ln1_g), as_row(ln1_b), w_up, w_down,
      as_row(ln2_g), as_row(ln2_b))


def kernel(x, w_in, w_gk2, b_gk, gla_norm_w, swa_sinks, w_out, ln1_g, ln1_b, w_up, w_down, ln2_g, ln2_b):
    bsz, seq, _ = x.shape
    n = bsz * seq
    x2 = x.reshape(n, D_MODEL)
    for l in range(DEPTH):
        gla, swa, wo, wu, wd = _mixer(x2, w_in[l], w_gk2[l], b_gk[l], gla_norm_w[l], swa_sinks[l], seq,
                                      w_out[l], w_up[l], w_down[l])
        x2 = _channel_mixer(gla, swa, x2, wo, ln1_g[l], ln1_b[l], wu, wd, ln2_g[l], ln2_b[l])
    return x2.reshape(bsz, seq, D_MODEL)
```
